```python
import jax, jax.numpy as jnp
from jax import lax
import numpy as np

D_MODEL = 1024
BATCH = 8
SEQ = 2048
DEPTH = 4

DN_HEADS = 4
DN_DK = 128
DN_DV = 128
DN_CHUNK = 64
SB_HEADS = 8
SB_DH = 64
SB_BLOCK = 128
LRU_WIDTH = 512
LRU_BLOCKS = 8
LRU_C = 8.0
CONV_W = 4
N_BRANCH = 3
N_EXPERTS = 16
N_GROUPS = 4
EXPERTS_PER_GROUP = N_EXPERTS // N_GROUPS
TOP_K = 2
D_FF = 512
DEEPNORM_ALPHA = (2.0 * DEPTH) ** 0.25
DEEPNORM_BETA = (8.0 * DEPTH) ** -0.25
LN_EPS = 1e-5
RMS_EPS = 1e-6

DN_QK = DN_HEADS * DN_DK
DN_V = DN_HEADS * DN_DV
SB_W = SB_HEADS * SB_DH
IN_SIZES = (DN_QK, DN_QK, DN_V, DN_V, DN_HEADS, DN_HEADS, SB_W, SB_W, SB_W, LRU_WIDTH, LRU_WIDTH)
IN_SPLITS = tuple(int(s) for s in np.cumsum(IN_SIZES)[:-1])
D_IN = int(sum(IN_SIZES))

kernel_name = 'hybrid_deltanet_stickbreak_rglru_grouped_moe'


def layer_norm(x, g, b):
    xf = x.astype(jnp.float32)
    mu = jnp.mean(xf, axis=-1, keepdims=True)
    xc = xf - mu
    var = jnp.mean(xc * xc, axis=-1, keepdims=True)
    return (xc * lax.rsqrt(var + LN_EPS) * g + b).astype(x.dtype)


def causal_depthwise_conv(x, w):
    return lax.conv_general_dilated(
        x, w[:, None, :], window_strides=(1,), padding=[(w.shape[0] - 1, 0)],
        dimension_numbers=('NWC', 'WIO', 'NWC'), feature_group_count=x.shape[-1])


def l2_normalize(x):
    xf = x.astype(jnp.float32)
    return xf * lax.rsqrt(jnp.sum(xf * xf, axis=-1, keepdims=True) + RMS_EPS)


def gated_rms_norm(o, z, w):
    of = o.astype(jnp.float32)
    n = of * lax.rsqrt(jnp.mean(of * of, axis=-1, keepdims=True) + RMS_EPS) * w
    return (n * jax.nn.silu(z.astype(jnp.float32))).astype(z.dtype)


def gated_delta_rule(q, k, v, g, beta):
    B, T, H, DK = q.shape
    DV = v.shape[-1]
    C = DN_CHUNK
    N = T // C
    f32 = jnp.float32

    def to_chunks(a):
        return a.astype(f32).reshape(B, N, C, H, -1).transpose(1, 0, 3, 2, 4)

    qc = to_chunks(q) * DK ** -0.5
    kc = to_chunks(k)
    vc = to_chunks(v)
    gc = to_chunks(g[..., None])[..., 0]
    bc = to_chunks(beta[..., None])[..., 0]
    g_cum = jnp.cumsum(gc, axis=-1)
    idx = jnp.arange(C)
    incl = idx[:, None] >= idx[None, :]
    strict = idx[:, None] > idx[None, :]
    diff = g_cum[..., :, None] - g_cum[..., None, :]
    decay = jnp.where(incl, jnp.exp(jnp.where(incl, diff, 0.0)), 0.0)
    kb = kc * bc[..., None]
    a_mat = jnp.where(strict, jnp.einsum('nbhid,nbhjd->nbhij', kb, kc) * decay, 0.0)
    eye = jnp.eye(C, dtype=f32)
    t_mat = lax.linalg.triangular_solve(a_mat + eye, jnp.broadcast_to(eye, a_mat.shape),
                                        left_side=True, lower=True, unit_diagonal=True)
    u = t_mat @ (vc * bc[..., None])
    w = t_mat @ (kb * jnp.exp(g_cum)[..., None])
    qk = jnp.einsum('nbhid,nbhjd->nbhij', qc, kc) * decay

    def step(S, inp):
        q_n, k_n, u_n, w_n, g_n, qk_n = inp
        v_new = u_n - w_n @ S
        o_n = (q_n * jnp.exp(g_n)[..., None]) @ S + qk_n @ v_new
        g_last = g_n[..., -1:]
        S = S * jnp.exp(g_last)[..., None] + jnp.einsum(
            'bhck,bhcv->bhkv', k_n * jnp.exp(g_last - g_n)[..., None], v_new)
        return S, o_n

    S0 = jnp.zeros((B, H, DK, DV), f32)
    _, o = lax.scan(step, S0, (qc, kc, u, w, g_cum, qk))
    return o.transpose(1, 0, 3, 2, 4).reshape(B, T, H, DV)


def stick_breaking_attention(q, k, v):
    B, T, H, Dh = q.shape
    nb = T // SB_BLOCK
    qb = q.reshape(B, nb, SB_BLOCK, H, Dh).transpose(1, 0, 3, 2, 4)
    kt = k.transpose(0, 2, 1, 3)
    vt = v.transpose(0, 2, 1, 3)
    key_pos = jnp.arange(T)

    def block(args):
        q_blk, blk = args
        z = jnp.einsum('bhqd,bhkd->bhqk', q_blk, kt).astype(jnp.float32) * Dh ** -0.5
        q_pos = blk * SB_BLOCK + jnp.arange(SB_BLOCK)
        past = key_pos[None, :] < q_pos[:, None]
        log_stay = jnp.where(past, jax.nn.log_sigmoid(-z), 0.0)
        log_after = lax.cumsum(log_stay, axis=3, reverse=True) - log_stay
        att = jnp.where(past, jnp.exp(jax.nn.log_sigmoid(z) + log_after), 0.0)
        return jnp.einsum('bhqk,bhkd->bhqd', att.astype(vt.dtype), vt)

    o = lax.map(block, (qb, jnp.arange(nb)))
    return o.transpose(1, 0, 3, 2, 4).reshape(B, T, H, Dh)


def rg_lru(x, w_a, b_a, w_x, b_x, lam):
    B, T, W = x.shape
    xh = x.reshape(B, T, LRU_BLOCKS, W // LRU_BLOCKS)
    gate_a = jnp.einsum('btnc,ncd->btnd', xh, w_a).reshape(B, T, W) + b_a
    gate_x = jnp.einsum('btnc,ncd->btnd', xh, w_x).reshape(B, T, W) + b_x
    r = jax.nn.sigmoid(gate_a.astype(jnp.float32))
    i = jax.nn.sigmoid(gate_x.astype(jnp.float32))
    log_a = -LRU_C * r * jax.nn.softplus(-lam.astype(jnp.float32))
    a = jnp.exp(log_a)
    b = jnp.sqrt(-jnp.expm1(2.0 * log_a)) * (i * x.astype(jnp.float32))

    def combine(left, right):
        a_l, b_l = left
        a_r, b_r = right
        return a_l * a_r, a_r * b_l + b_r

    _, h = lax.associative_scan(combine, (a, b), axis=1)
    return h.astype(x.dtype)


def hybrid_mixer(x, w_in, dn_conv_w, dn_a_log, dn_dt_bias, dn_norm_w, lru_conv_w, lru_conv_b,
                 lru_w_a, lru_b_a, lru_w_x, lru_b_x, lru_lambda, w_up_dn, w_up_sb, w_up_lru,
                 w_merge, b_merge, w_out):
    B, T, D = x.shape
    proj = x @ w_in
    (dn_q, dn_k, dn_v, dn_z, dn_b, dn_a, sb_q, sb_k, sb_v, lru_x, lru_y) = jnp.split(proj, IN_SPLITS, axis=-1)

    qkv = jax.nn.silu(causal_depthwise_conv(jnp.concatenate([dn_q, dn_k, dn_v], axis=-1), dn_conv_w))
    q, k, v = jnp.split(qkv, (DN_QK, 2 * DN_QK), axis=-1)
    q = l2_normalize(q.reshape(B, T, DN_HEADS, DN_DK))
    k = l2_normalize(k.reshape(B, T, DN_HEADS, DN_DK))
    v = v.reshape(B, T, DN_HEADS, DN_DV)
    beta = jax.nn.sigmoid(dn_b.astype(jnp.float32))
    g = -jnp.exp(dn_a_log.astype(jnp.float32)) * jax.nn.softplus(dn_a.astype(jnp.float32) + dn_dt_bias)
    o_a = gated_delta_rule(q, k, v, g, beta)
    y_a = gated_rms_norm(o_a, dn_z.reshape(B, T, DN_HEADS, DN_DV), dn_norm_w).reshape(B, T, DN_V)

    y_b = stick_breaking_attention(sb_q.reshape(B, T, SB_HEADS, SB_DH), sb_k.reshape(B, T, SB_HEADS, SB_DH),
                                   sb_v.reshape(B, T, SB_HEADS, SB_DH)).reshape(B, T, SB_W)

    xc = causal_depthwise_conv(lru_x, lru_conv_w) + lru_conv_b
    y_c = jax.nn.gelu(lru_y) * rg_lru(xc, lru_w_a, lru_b_a, lru_w_x, lru_b_x, lru_lambda)

    gates = jax.nn.sigmoid((x @ w_merge + b_merge).astype(jnp.float32)).reshape(B, T, N_BRANCH, D)
    merged = (gates[:, :, 0] * (y_a @ w_up_dn) + gates[:, :, 1] * (y_b @ w_up_sb)
              + gates[:, :, 2] * (y_c @ w_up_lru)).astype(x.dtype)
    return merged @ w_out


def grouped_moe(x, w_router, b_router, w1, w3, w2):
    B, T, D = x.shape
    N = B * T
    xf = x.reshape(N, D)
    scores = jax.nn.softmax((xf @ w_router).astype(jnp.float32), axis=-1)
    sel = (scores + b_router).reshape(N, N_GROUPS, EXPERTS_PER_GROUP)
    group_score = jnp.sum(lax.top_k(sel, TOP_K)[0], axis=-1)
    g_idx = jnp.argmax(group_score, axis=-1)
    in_group = jnp.take_along_axis(sel, g_idx[:, None, None], axis=1)[:, 0]
    _, e_local = lax.top_k(in_group, TOP_K)
    e_idx = g_idx[:, None] * EXPERTS_PER_GROUP + e_local
    w_sel = jnp.take_along_axis(scores, e_idx, axis=-1)
    w_sel = w_sel / jnp.sum(w_sel, axis=-1, keepdims=True)
    gate = jnp.sum(jax.nn.one_hot(e_idx, N_EXPERTS, dtype=jnp.float32) * w_sel[..., None], axis=1)
    h = jax.nn.silu(jnp.einsum('nd,edf->nef', xf, w1)) * jnp.einsum('nd,edf->nef', xf, w3)
    y = jnp.einsum('nef,efd->nd', (h * gate[:, :, None]).astype(x.dtype), w2)
    return y.reshape(B, T, D)


def setup_inputs(seed: int = 0) -> dict:
    key = jax.random.key(seed)
    ks = jax.random.split(key, 32)
    f32 = jnp.float32
    L = DEPTH

    def nrm(k, shape, scale):
        return jax.random.normal(k, shape, f32) * scale

    dt = jnp.exp(jax.random.uniform(ks[4], (L, DN_HEADS), f32, np.log(1e-3), np.log(1e-1)))
    u = jax.random.uniform(ks[12], (L, LRU_WIDTH), f32, 0.9, 0.999)
    s = u ** (1.0 / LRU_C)
    bw = LRU_WIDTH // LRU_BLOCKS
    return {
        'x': nrm(ks[0], (BATCH, SEQ, D_MODEL), 1.0),
        'w_in': nrm(ks[1], (L, D_MODEL, D_IN), D_MODEL ** -0.5),
        'dn_conv_w': nrm(ks[2], (L, CONV_W, 2 * DN_QK + DN_V), CONV_W ** -0.5),
        'dn_a_log': jnp.log(jax.random.uniform(ks[3], (L, DN_HEADS), f32, 1.0, 16.0)),
        'dn_dt_bias': dt + jnp.log(-jnp.expm1(-dt)),
        'dn_norm_w': 1.0 + nrm(ks[5], (L, DN_DV), 0.02),
        'lru_conv_w': nrm(ks[6], (L, CONV_W, LRU_WIDTH), CONV_W ** -0.5),
        'lru_conv_b': nrm(ks[7], (L, LRU_WIDTH), 0.01),
        'lru_w_a': nrm(ks[8], (L, LRU_BLOCKS, bw, bw), bw ** -0.5),
        'lru_b_a': nrm(ks[9], (L, LRU_WIDTH), 0.01),
        'lru_w_x': nrm(ks[10], (L, LRU_BLOCKS, bw, bw), bw ** -0.5),
        'lru_b_x': nrm(ks[11], (L, LRU_WIDTH), 0.01),
        'lru_lambda': jnp.log(s) - jnp.log1p(-s),
        'w_up_dn': nrm(ks[13], (L, DN_V, D_MODEL), DN_V ** -0.5),
        'w_up_sb': nrm(ks[14], (L, SB_W, D_MODEL), SB_W ** -0.5),
        'w_up_lru': nrm(ks[15], (L, LRU_WIDTH, D_MODEL), LRU_WIDTH ** -0.5),
        'w_merge': nrm(ks[16], (L, D_MODEL, N_BRANCH * D_MODEL), D_MODEL ** -0.5),
        'b_merge': nrm(ks[17], (L, N_BRANCH * D_MODEL), 0.01),
        'w_out': nrm(ks[18], (L, D_MODEL, D_MODEL), D_MODEL ** -0.5 * DEEPNORM_BETA),
        'ln1_g': 1.0 + nrm(ks[19], (L, D_MODEL), 0.02),
        'ln1_b': nrm(ks[20], (L, D_MODEL), 0.02),
        'w_router': nrm(ks[21], (D_MODEL, N_EXPERTS), D_MODEL ** -0.5),
        'b_router': nrm(ks[22], (N_EXPERTS,), 0.01),
        'moe_w1': nrm(ks[23], (L, N_EXPERTS, D_MODEL, D_FF), D_MODEL ** -0.5),
        'moe_w3': nrm(ks[24], (L, N_EXPERTS, D_MODEL, D_FF), D_MODEL ** -0.5),
        'moe_w2': nrm(ks[25], (L, N_EXPERTS, D_FF, D_MODEL), D_FF ** -0.5 * DEEPNORM_BETA),
        'ln2_g': 1.0 + nrm(ks[26], (L, D_MODEL), 0.02),
        'ln2_b': nrm(ks[27], (L, D_MODEL), 0.02),
    }


def reference(x, w_in, dn_conv_w, dn_a_log, dn_dt_bias, dn_norm_w, lru_conv_w, lru_conv_b, lru_w_a, lru_b_a,
              lru_w_x, lru_b_x, lru_lambda, w_up_dn, w_up_sb, w_up_lru, w_merge, b_merge, w_out, ln1_g, ln1_b,
              w_router, b_router, moe_w1, moe_w3, moe_w2, ln2_g, ln2_b):
    for l in range(DEPTH):
        mix = hybrid_mixer(x, w_in[l], dn_conv_w[l], dn_a_log[l], dn_dt_bias[l], dn_norm_w[l], lru_conv_w[l],
                           lru_conv_b[l], lru_w_a[l], lru_b_a[l], lru_w_x[l], lru_b_x[l], lru_lambda[l],
                           w_up_dn[l], w_up_sb[l], w_up_lru[l], w_merge[l], b_merge[l], w_out[l])
        x = layer_norm(DEEPNORM_ALPHA * x + mix, ln1_g[l], ln1_b[l])
        ffn = grouped_moe(x, w_router, b_router, moe_w1[l], moe_w3[l], moe_w2[l])
        x = layer_norm(DEEPNORM_ALPHA * x + ffn, ln2_g[l], ln2_b[l])
    return x
```

```python
import functools

import jax
import jax.numpy as jnp
from jax import lax
from jax.experimental import pallas as pl
from jax.experimental.pallas import tpu as pltpu

F32 = jnp.float32
BF16 = jnp.bfloat16

D_MODEL = 1024
DEPTH = 4
DN_HEADS = 4
DN_DK = 128
DN_CHUNK = 64
DN_W = DN_HEADS * DN_DK
SB_HEADS = 8
SB_DH = 64
SB_W = SB_HEADS * SB_DH
LRU_W = 512
LRU_BLOCKS = 8
LRU_C = 8.0
CONV_W = 4
N_EXPERTS = 16
N_GROUPS = 4
EPG = N_EXPERTS // N_GROUPS
D_FF = 512
ALPHA = (2.0 * DEPTH) ** 0.25
LN_EPS = 1e-5
RMS_EPS = 1e-6

C_DN = 0
C_MG = 2048
C_SBQ = 5120
C_SBK = 5632
C_SBV = 6144
C_LX = 6656
C_LY = 7168
P_COLS = 7680
IN_OFF = {"dn": 0, "ba": 2048, "sb": 2056, "lru": 3592}

LANES = 128
HALO = 8
VMEM_LIMIT = 48 * 1024 * 1024


def _cparams(sem):
    return pltpu.CompilerParams(dimension_semantics=sem, vmem_limit_bytes=VMEM_LIMIT)


def _dot(a, b):
    return jnp.dot(a, b, preferred_element_type=F32)


def _dot_nt(a, b):
    return lax.dot_general(a, b, (((1,), (1,)), ((), ())), preferred_element_type=F32)


def _dot_tn(a, b):
    return lax.dot_general(a, b, (((0,), (0,)), ((), ())), preferred_element_type=F32)


def _split3(a):
    hi = a.astype(BF16)
    r1 = a - hi.astype(F32)
    mid = r1.astype(BF16)
    lo = (r1 - mid.astype(F32)).astype(BF16)
    return hi, mid, lo


def _sigmoid(x):
    return 1.0 / (1.0 + jnp.exp(-x))


def _softplus(x):
    return jnp.maximum(x, 0.0) + jnp.log(1.0 + jnp.exp(-jnp.abs(x)))


def _layer_norm(h, g, b):
    mu = jnp.mean(h, axis=-1, keepdims=True)
    hc = h - mu
    var = jnp.mean(hc * hc, axis=-1, keepdims=True)
    return hc * lax.rsqrt(var + LN_EPS) * g + b


def _proj_kernel(x_ref, w_ref, wba_ref, wbat_ref, p_ref, ba_ref, bat_ref, xb_ref):
    @pl.when(pl.program_id(1) == 0)
    def _():
        xb = x_ref[...].astype(BF16)
        xb_ref[...] = xb
        ba_ref[...] = _dot(xb, wba_ref[...])
        bat_ref[...] = _dot_nt(wbat_ref[...], xb)

    p_ref[...] = _dot(xb_ref[...], w_ref[...]).astype(p_ref.dtype)


def _proj(x, w_all, w_ba, w_bat, tm, tn):
    n = x.shape[0]
    return pl.pallas_call(
        _proj_kernel,
        grid=(n // tm, P_COLS // tn),
        in_specs=[
            pl.BlockSpec((tm, D_MODEL), lambda i, j: (i, 0)),
            pl.BlockSpec((D_MODEL, tn), lambda i, j: (0, j)),
            pl.BlockSpec((D_MODEL, LANES), lambda i, j: (0, 0)),
            pl.BlockSpec((16, D_MODEL), lambda i, j: (0, 0)),
        ],
        out_specs=[
            pl.BlockSpec((tm, tn), lambda i, j: (i, j)),
            pl.BlockSpec((tm, LANES), lambda i, j: (i, 0)),
            pl.BlockSpec((16, tm), lambda i, j: (0, i)),
        ],
        out_shape=[
            jax.ShapeDtypeStruct((n, P_COLS), F32),
            jax.ShapeDtypeStruct((n, LANES), F32),
            jax.ShapeDtypeStruct((16, n), F32),
        ],
        scratch_shapes=[pltpu.VMEM((tm, D_MODEL), BF16)],
        compiler_params=_cparams(("parallel", "arbitrary")),
        name="proj",
    )(x, w_all, w_ba, w_bat)


def _conv_from_pad(xpad_ref, cw, r, rows):
    acc = cw[CONV_W - 1:CONV_W, :] * xpad_ref[r + HALO:r + HALO + rows, :]
    for j in range(CONV_W - 1):
        off = r + HALO - (CONV_W - 1) + j
        acc = acc + cw[j:j + 1, :] * xpad_ref[off:off + rows, :]
    return acc


def _load_halo_block(xpad_ref, blk, tb, first):
    @pl.when(first)
    def _():
        xpad_ref[0:HALO, :] = jnp.zeros((HALO, xpad_ref.shape[1]), F32)

    @pl.when(jnp.logical_not(first))
    def _():
        xpad_ref[0:HALO, :] = xpad_ref[tb:tb + HALO, :]

    xpad_ref[HALO:tb + HALO, :] = blk


def _dn_kernel(pd_ref, ba_ref, bat_ref, cw_ref, prow_ref, pcol_ref, nw_ref, ya_ref,
               xpad_ref, qkv_ref, s_ref, gcc_ref, gct_ref, *, tb):
    C = DN_CHUNK
    nc = tb // C
    first = pl.program_id(1) == 0

    @pl.when(first)
    def _():
        s_ref[...] = jnp.zeros_like(s_ref)

    _load_halo_block(xpad_ref, pd_ref[:, 0:3 * DN_W].astype(F32), tb, first)

    cw = cw_ref[...]
    for r in range(0, tb, C):
        y = _conv_from_pad(xpad_ref, cw, r, C)
        y = y * _sigmoid(y)
        for h in range(2 * DN_HEADS):
            seg = y[:, h * DN_DK:(h + 1) * DN_DK]
            nrm = lax.rsqrt(jnp.sum(seg * seg, axis=-1, keepdims=True) + RMS_EPS)
            if h < DN_HEADS:
                nrm = nrm * (DN_DK ** -0.5)
            qkv_ref[r:r + C, h * DN_DK:(h + 1) * DN_DK] = seg * nrm
        qkv_ref[r:r + C, 2 * DN_W:3 * DN_W] = y[:, 2 * DN_W:3 * DN_W]

    g_col = -jnp.exp(prow_ref[0:1, :]) * _softplus(ba_ref[...] + prow_ref[1:2, :])
    g_row = -jnp.exp(pcol_ref[:, 0:1]) * _softplus(bat_ref[...] + pcol_ref[:, 1:2])
    ri = lax.broadcasted_iota(jnp.int32, (tb, tb), 0)
    ci = lax.broadcasted_iota(jnp.int32, (tb, tb), 1)
    same = (ri // C) == (ci // C)
    low = jnp.where(same & (ri >= ci), 1.0, 0.0).astype(BF16)
    upp = jnp.where(same & (ri <= ci), 1.0, 0.0).astype(BF16)
    c0, c1, c2 = _split3(g_col)
    gcc_ref[...] = _dot(low, c0) + _dot(low, c1) + _dot(low, c2)
    r0_, r1_, r2_ = _split3(g_row)
    gct = _dot(r0_, upp) + _dot(r1_, upp) + _dot(r2_, upp)
    for c in range(nc):
        gct_ref[c] = gct[:, c * C:(c + 1) * C]

    rr = lax.broadcasted_iota(jnp.int32, (C, C), 0)
    cc = lax.broadcasted_iota(jnp.int32, (C, C), 1)
    incl = rr >= cc
    strict = rr > cc
    eye = jnp.where(rr == cc, 1.0, 0.0).astype(F32)
    nw = nw_ref[...]

    def chunk(c, carry):
        r0 = pl.multiple_of(c * C, C)
        bsig = _sigmoid(ba_ref[pl.ds(r0, C), :])
        gcc_c = gcc_ref[pl.ds(r0, C), :]
        gct_c = gct_ref[c]
        for h in range(DN_HEADS):
            q = qkv_ref[pl.ds(r0, C), h * DN_DK:(h + 1) * DN_DK]
            k = qkv_ref[pl.ds(r0, C), DN_W + h * DN_DK:DN_W + (h + 1) * DN_DK]
            v = qkv_ref[pl.ds(r0, C), 2 * DN_W + h * DN_DK:2 * DN_W + (h + 1) * DN_DK]
            beta = bsig[:, h:h + 1]
            gc = gcc_c[:, 4 + h:5 + h]
            gr = gct_c[4 + h:5 + h, :]
            decay = jnp.where(incl, jnp.exp(jnp.where(incl, gc - gr, 0.0)), 0.0)
            kb = k * beta
            kbf = k.astype(BF16)
            a_mat = jnp.where(strict, _dot_nt(kb.astype(BF16), kbf) * decay, 0.0)
            pw = -a_mat
            t_mat = eye + pw
            for _ in range(5):
                pb = pw.astype(BF16)
                pw = _dot(pb, pb)
                t_mat = t_mat + _dot(t_mat.astype(BF16), pw.astype(BF16))
            eg = jnp.exp(gc)
            rhs = jnp.concatenate([v * beta, kb * eg], axis=1).astype(BF16)
            uw = _dot(t_mat.astype(BF16), rhs)
            u = uw[:, 0:DN_DK]
            w = uw[:, DN_DK:2 * DN_DK]
            qk = _dot_nt(q.astype(BF16), kbf) * decay
            s_old = s_ref[h]
            sb = s_old.astype(BF16)
            v_new = u - _dot(w.astype(BF16), sb)
            vnb = v_new.astype(BF16)
            o = _dot((q * eg).astype(BF16), sb) + _dot(qk.astype(BF16), vnb)
            g_last = gr[:, C - 1:C]
            kd = (k * jnp.exp(g_last - gc)).astype(BF16)
            s_ref[h] = s_old * jnp.exp(g_last) + _dot_tn(kd, vnb)
            z = pd_ref[pl.ds(r0, C), 3 * DN_W + h * DN_DK:3 * DN_W + (h + 1) * DN_DK].astype(F32)
            n = o * lax.rsqrt(jnp.mean(o * o, axis=-1, keepdims=True) + RMS_EPS) * nw
            ya_ref[pl.ds(r0, C), h * DN_DK:(h + 1) * DN_DK] = n * (z * _sigmoid(z))
        return carry

    lax.fori_loop(0, nc, chunk, 0)


def _deltanet(p, ba, bat, cw, prow, pcol, nw, batch, seq, tb):
    n = batch * seq
    nt = seq // tb
    return pl.pallas_call(
        functools.partial(_dn_kernel, tb=tb),
        grid=(batch, nt),
        in_specs=[
            pl.BlockSpec((tb, 4 * DN_W), lambda b, t: (b * nt + t, C_DN // (4 * DN_W))),
            pl.BlockSpec((tb, LANES), lambda b, t: (b * nt + t, 0)),
            pl.BlockSpec((16, tb), lambda b, t: (0, b * nt + t)),
            pl.BlockSpec((CONV_W, 3 * DN_W), lambda b, t: (0, 0)),
            pl.BlockSpec((8, LANES), lambda b, t: (0, 0)),
            pl.BlockSpec((16, LANES), lambda b, t: (0, 0)),
            pl.BlockSpec((1, DN_DK), lambda b, t: (0, 0)),
        ],
        out_specs=pl.BlockSpec((tb, DN_W), lambda b, t: (b * nt + t, 0)),
        out_shape=jax.ShapeDtypeStruct((n, DN_W), F32),
        scratch_shapes=[
            pltpu.VMEM((tb + HALO, 3 * DN_W), F32),
            pltpu.VMEM((tb, 3 * DN_W), F32),
            pltpu.VMEM((DN_HEADS, DN_DK, DN_DK), F32),
            pltpu.VMEM((tb, LANES), F32),
            pltpu.VMEM((tb // DN_CHUNK, 16, DN_CHUNK), F32),
        ],
        compiler_params=_cparams(("parallel", "arbitrary")),
        name="deltanet",
    )(p, ba, bat, cw, prow, pcol, nw)


def _sb_kernel(q_ref, k_ref, v_ref, o_ref, *, tq):
    qi = pl.program_id(2)
    lane = lax.broadcasted_iota(jnp.int32, (tq, LANES), 1)
    q = q_ref[...].astype(F32) * (SB_DH ** -0.5)
    ri = lax.broadcasted_iota(jnp.int32, (tq, tq), 0)
    ci = lax.broadcasted_iota(jnp.int32, (tq, tq), 1)
    past = ci < ri
    suf1 = jnp.where(ri > ci, 1.0, 0.0).astype(BF16)
    suf = jnp.concatenate([suf1, suf1], axis=0)

    out = jnp.zeros((tq, LANES), F32)
    for h in range(2):
        hm = (lane < SB_DH) if h == 0 else (lane >= SB_DH)
        qh = jnp.where(hm, q, 0.0).astype(BF16)

        def blk(kb, carry, acc, masked, qh=qh):
            k0 = pl.multiple_of(kb * tq, tq)
            kk = k_ref[pl.ds(k0, tq), :].astype(BF16)
            vv = v_ref[pl.ds(k0, tq), :].astype(BF16)
            z = _dot_nt(qh, kk)
            sp = _softplus(z)
            ls = -sp
            if masked:
                ls = jnp.where(past, ls, 0.0)
            hi = ls.astype(BF16)
            lo = (ls - hi.astype(F32)).astype(BF16)
            after = _dot(jnp.concatenate([hi, lo], axis=1), suf)
            att = jnp.exp((z - sp) + after + carry)
            if masked:
                att = jnp.where(past, att, 0.0)
            acc = acc + _dot(att.astype(BF16), vv)
            carry = carry + jnp.sum(ls, axis=1, keepdims=True)
            return carry, acc

        carry, acc = blk(qi, jnp.zeros((tq, 1), F32), jnp.zeros((tq, LANES), F32), True)
        carry, acc = lax.fori_loop(
            0, qi, lambda it, ca: blk(qi - 1 - it, ca[0], ca[1], False), (carry, acc))
        out = jnp.where(hm, acc, out)
    o_ref[...] = out


def _stickbreak(p, batch, seq, tq):
    n = batch * seq
    nq = seq // tq
    hp = SB_W // LANES
    return pl.pallas_call(
        functools.partial(_sb_kernel, tq=tq),
        grid=(batch, hp, nq),
        in_specs=[
            pl.BlockSpec((tq, LANES), lambda b, h, i: (b * nq + i, C_SBQ // LANES + h)),
            pl.BlockSpec((seq, LANES), lambda b, h, i: (b, C_SBK // LANES + h)),
            pl.BlockSpec((seq, LANES), lambda b, h, i: (b, C_SBV // LANES + h)),
        ],
        out_specs=pl.BlockSpec((tq, LANES), lambda b, h, i: (b * nq + i, h)),
        out_shape=jax.ShapeDtypeStruct((n, SB_W), F32),
        compiler_params=_cparams(("parallel", "parallel", "arbitrary")),
        name="stickbreak",
    )(p, p, p)


def _lru_kernel(px_ref, py_ref, cw_ref, cb_ref, wa_ref, ba_ref, wx_ref, bx_ref, lam_ref, yc_ref,
                xpad_ref, a_ref, h_ref, car_ref, *, tb):
    first = pl.program_id(1) == 0

    @pl.when(first)
    def _():
        car_ref[...] = jnp.zeros_like(car_ref)

    _load_halo_block(xpad_ref, px_ref[...].astype(F32), tb, first)
    xc = _conv_from_pad(xpad_ref, cw_ref[...], 0, tb) + cb_ref[...]
    xcb = xc.astype(BF16)
    r = _sigmoid(_dot(xcb, wa_ref[...]) + ba_ref[...])
    i = _sigmoid(_dot(xcb, wx_ref[...]) + bx_ref[...])
    log_a = -LRU_C * r * _softplus(-lam_ref[...])
    a_ref[...] = jnp.exp(log_a)
    h_ref[...] = jnp.sqrt(1.0 - jnp.exp(2.0 * log_a)) * (i * xc)

    row = lax.broadcasted_iota(jnp.int32, (8, LRU_W), 0)

    def tile(j, h):
        r0 = pl.multiple_of(j * 8, 8)
        a = a_ref[pl.ds(r0, 8), :]
        b = h_ref[pl.ds(r0, 8), :]
        for s in (1, 2, 4):
            m = row >= s
            b = jnp.where(m, a * pltpu.roll(b, s, axis=0) + b, b)
            a = jnp.where(m, a * pltpu.roll(a, s, axis=0), a)
        hh = a * h + b
        h_ref[pl.ds(r0, 8), :] = hh
        return hh[7:8, :]

    car_ref[0:1, :] = lax.fori_loop(0, tb // 8, tile, car_ref[0:1, :])
    y = py_ref[...].astype(F32)
    gelu = 0.5 * y * (1.0 + jnp.tanh(0.7978845608028654 * (y + 0.044715 * (y * y * y))))
    yc_ref[...] = gelu * h_ref[...]


def _rglru(p, cw, cb, wa, ba, wx, bx, lam, batch, seq, tb):
    n = batch * seq
    nt = seq // tb
    vec = pl.BlockSpec((1, LRU_W), lambda b, t: (0, 0))
    mat = pl.BlockSpec((LRU_W, LRU_W), lambda b, t: (0, 0))
    return pl.pallas_call(
        functools.partial(_lru_kernel, tb=tb),
        grid=(batch, nt),
        in_specs=[
            pl.BlockSpec((tb, LRU_W), lambda b, t: (b * nt + t, C_LX // LRU_W)),
            pl.BlockSpec((tb, LRU_W), lambda b, t: (b * nt + t, C_LY // LRU_W)),
            pl.BlockSpec((CONV_W, LRU_W), lambda b, t: (0, 0)),
            vec, mat, vec, mat, vec, vec,
        ],
        out_specs=pl.BlockSpec((tb, LRU_W), lambda b, t: (b * nt + t, 0)),
        out_shape=jax.ShapeDtypeStruct((n, LRU_W), F32),
        scratch_shapes=[
            pltpu.VMEM((tb + HALO, LRU_W), F32),
            pltpu.VMEM((tb, LRU_W), F32),
            pltpu.VMEM((tb, LRU_W), F32),
            pltpu.VMEM((8, LRU_W), F32),
        ],
        compiler_params=_cparams(("parallel", "arbitrary")),
        name="rglru",
    )(p, p, cw, cb, wa, ba, wx, bx, lam)


def _top2_sum(a, b, c, d):
    m1, n1 = jnp.maximum(a, b), jnp.minimum(a, b)
    m2, n2 = jnp.maximum(c, d), jnp.minimum(c, d)
    return jnp.maximum(m1, m2) + jnp.maximum(jnp.minimum(m1, m2), jnp.maximum(n1, n2))


def _merge_kernel(x_ref, g0_ref, g1_ref, g2_ref, ya_ref, yb_ref, yc_ref, wa_ref, wb_ref, wc_ref, wo_ref,
                  bm_ref, lng_ref, lnb_ref, wrt_ref, br_ref, x1_ref, gate_ref, gt_ref):
    merged = (_sigmoid(g0_ref[...].astype(F32) + bm_ref[0:1, :]) * _dot(ya_ref[...].astype(BF16), wa_ref[...])
              + _sigmoid(g1_ref[...].astype(F32) + bm_ref[1:2, :]) * _dot(yb_ref[...].astype(BF16), wb_ref[...])
              + _sigmoid(g2_ref[...].astype(F32) + bm_ref[2:3, :]) * _dot(yc_ref[...].astype(BF16), wc_ref[...]))
    mix = _dot(merged.astype(BF16), wo_ref[...])
    x1 = _layer_norm(ALPHA * x_ref[...] + mix, lng_ref[...], lnb_ref[...])
    x1_ref[...] = x1

    lt = _dot_nt(wrt_ref[...], x1.astype(BF16))
    ex = jnp.exp(lt - jnp.max(lt, axis=0, keepdims=True))
    score = ex / jnp.sum(ex, axis=0, keepdims=True)
    sel = score + br_ref[:, 0:1]
    sc = [score[e:e + 1, :] for e in range(N_EXPERTS)]
    sl = [sel[e:e + 1, :] for e in range(N_EXPERTS)]

    best = _top2_sum(*sl[0:EPG])
    gidx = jnp.zeros_like(best, dtype=jnp.int32)
    for g in range(1, N_GROUPS):
        gs = _top2_sum(*sl[g * EPG:(g + 1) * EPG])
        better = gs > best
        gidx = jnp.where(better, g, gidx)
        best = jnp.where(better, gs, best)

    def pick(vals, j):
        out = vals[j]
        for g in range(1, N_GROUPS):
            out = jnp.where(gidx == g, vals[g * EPG + j], out)
        return out

    vs = [pick(sl, j) for j in range(EPG)]
    ws = [pick(sc, j) for j in range(EPG)]
    b1, i1, w1 = vs[0], jnp.zeros_like(gidx), ws[0]
    for j in range(1, EPG):
        gt = vs[j] > b1
        i1 = jnp.where(gt, j, i1)
        w1 = jnp.where(gt, ws[j], w1)
        b1 = jnp.where(gt, vs[j], b1)
    b2 = jnp.full_like(b1, -jnp.inf)
    i2 = jnp.zeros_like(gidx)
    w2 = jnp.zeros_like(w1)
    for j in range(EPG):
        gt = (i1 != j) & (vs[j] > b2)
        i2 = jnp.where(gt, j, i2)
        w2 = jnp.where(gt, ws[j], w2)
        b2 = jnp.where(gt, vs[j], b2)
    den = w1 + w2
    w1n = w1 / den
    w2n = w2 / den

    gt_ref[...] = jnp.zeros_like(gt_ref)
    for g in range(N_GROUPS):
        ing = gidx == g
        for j in range(EPG):
            e = g * EPG + j
            gt_ref[e:e + 1, :] = (jnp.where(ing & (i1 == j), w1n, 0.0) + jnp.where(ing & (i2 == j), w2n, 0.0))
    gate_ref[...] = gt_ref[...].T


def _merge(x, p, ya, yb, yc, wa, wb, wc, wo, bm, lng, lnb, wrt, br, tm):
    n = x.shape[0]
    row = lambda w: pl.BlockSpec((tm, w), lambda i: (i, 0))
    full = lambda a: pl.BlockSpec(a.shape, lambda i: (0, 0))
    mg = C_MG // D_MODEL
    return pl.pallas_call(
        _merge_kernel,
        grid=(n // tm,),
        in_specs=[
            row(D_MODEL),
            pl.BlockSpec((tm, D_MODEL), lambda i: (i, mg)),
            pl.BlockSpec((tm, D_MODEL), lambda i: (i, mg + 1)),
            pl.BlockSpec((tm, D_MODEL), lambda i: (i, mg + 2)),
            row(DN_W), row(SB_W), row(LRU_W),
            full(wa), full(wb), full(wc), full(wo), full(bm), full(lng), full(lnb), full(wrt), full(br),
        ],
        out_specs=[row(D_MODEL), row(LANES)],
        out_shape=[jax.ShapeDtypeStruct((n, D_MODEL), F32), jax.ShapeDtypeStruct((n, LANES), F32)],
        scratch_shapes=[pltpu.VMEM((LANES, tm), F32)],
        compiler_params=_cparams(("parallel",)),
        name="merge",
    )(x, p, p, p, ya, yb, yc, wa, wb, wc, wo, bm, lng, lnb, wrt, br)


def _moe_kernel(x_ref, gate_ref, w1_ref, w3_ref, w2_ref, lng_ref, lnb_ref, o_ref, xb_ref, acc_ref):
    e = pl.program_id(1)

    @pl.when(e == 0)
    def _():
        xb_ref[...] = x_ref[...].astype(BF16)
        acc_ref[...] = jnp.zeros_like(acc_ref)

    xb = xb_ref[...]
    h1 = _dot(xb, w1_ref[0])
    h3 = _dot(xb, w3_ref[0])
    lane = lax.broadcasted_iota(jnp.int32, gate_ref.shape, 1)
    ge = jnp.sum(jnp.where(lane == e, gate_ref[...], 0.0), axis=1, keepdims=True)
    hh = (h1 * _sigmoid(h1)) * h3 * ge
    acc_ref[...] += _dot(hh.astype(BF16), w2_ref[0])

    @pl.when(e == N_EXPERTS - 1)
    def _():
        o_ref[...] = _layer_norm(ALPHA * x_ref[...] + acc_ref[...], lng_ref[...], lnb_ref[...])


def _moe(x1, gate, w1, w3, w2, lng, lnb, tm):
    n = x1.shape[0]
    vec = pl.BlockSpec((1, D_MODEL), lambda i, e: (0, 0))
    return pl.pallas_call(
        _moe_kernel,
        grid=(n // tm, N_EXPERTS),
        in_specs=[
            pl.BlockSpec((tm, D_MODEL), lambda i, e: (i, 0)),
            pl.BlockSpec((tm, LANES), lambda i, e: (i, 0)),
            pl.BlockSpec((1, D_MODEL, D_FF), lambda i, e: (e, 0, 0)),
            pl.BlockSpec((1, D_MODEL, D_FF), lambda i, e: (e, 0, 0)),
            pl.BlockSpec((1, D_FF, D_MODEL), lambda i, e: (e, 0, 0)),
            vec, vec,
        ],
        out_specs=pl.BlockSpec((tm, D_MODEL), lambda i, e: (i, 0)),
        out_shape=jax.ShapeDtypeStruct((n, D_MODEL), F32),
        scratch_shapes=[pltpu.VMEM((tm, D_MODEL), BF16), pltpu.VMEM((tm, D_MODEL), F32)],
        compiler_params=_cparams(("parallel", "arbitrary")),
        name="moe",
    )(x1, gate, w1, w3, w2, lng, lnb)


def _pad_rows(a, rows):
    return jnp.pad(a, ((0, rows - a.shape[0]), (0, 0)))


def _block_diag(w):
    nb, c, _ = w.shape
    eye = jnp.eye(nb, dtype=w.dtype)
    return jnp.einsum("ncd,nm->ncmd", w, eye).reshape(nb * c, nb * c)


def kernel(x, w_in, dn_conv_w, dn_a_log, dn_dt_bias, dn_norm_w, lru_conv_w, lru_conv_b, lru_w_a, lru_b_a,
           lru_w_x, lru_b_x, lru_lambda, w_up_dn, w_up_sb, w_up_lru, w_merge, b_merge, w_out, ln1_g, ln1_b,
           w_router, b_router, moe_w1, moe_w3, moe_w2, ln2_g, ln2_b):
    batch, seq, _ = x.shape
    n = batch * seq
    tb = min(512, seq)
    tq = min(128, seq)
    tm_proj = min(1024, n)
    tm_merge = min(256, n)
    tm_moe = min(1024, n)
    xf = x.reshape(n, D_MODEL)

    wrt = _pad_rows(w_router.T, 16).astype(BF16)
    br = jnp.pad(b_router.reshape(N_EXPERTS, 1), ((0, 0), (0, LANES - 1)))

    for l in range(DEPTH):
        wi = w_in[l]
        w_all = jnp.concatenate(
            [wi[:, IN_OFF["dn"]:IN_OFF["ba"]], w_merge[l], wi[:, IN_OFF["sb"]:IN_OFF["lru"]], wi[:, IN_OFF["lru"]:]],
            axis=1).astype(BF16)
        w_ba8 = wi[:, IN_OFF["ba"]:IN_OFF["sb"]]
        w_ba = jnp.pad(w_ba8, ((0, 0), (0, LANES - 8))).astype(BF16)
        w_bat = _pad_rows(w_ba8.T, 16).astype(BF16)
        p, ba, bat = _proj(xf, w_all, w_ba, w_bat, tm_proj, 512)

        pvec = jnp.stack([dn_a_log[l], dn_dt_bias[l]])
        prow = jnp.pad(pvec, ((0, 6), (4, LANES - 8)))
        pcol = jnp.pad(pvec.T, ((4, 8), (0, LANES - 2)))
        ya = _deltanet(p, ba, bat, dn_conv_w[l], prow, pcol, dn_norm_w[l].reshape(1, DN_DK), batch, seq, tb)

        yb = _stickbreak(p, batch, seq, tq)

        yc = _rglru(p, lru_conv_w[l], lru_conv_b[l].reshape(1, LRU_W),
                    _block_diag(lru_w_a[l]).astype(BF16), lru_b_a[l].reshape(1, LRU_W),
                    _block_diag(lru_w_x[l]).astype(BF16), lru_b_x[l].reshape(1, LRU_W),
                    lru_lambda[l].reshape(1, LRU_W), batch, seq, tb)

        x1, gate = _merge(xf, p, ya, yb, yc,
                          w_up_dn[l].astype(BF16), w_up_sb[l].astype(BF16), w_up_lru[l].astype(BF16),
                          w_out[l].astype(BF16), _pad_rows(b_merge[l].reshape(3, D_MODEL), 8),
                          ln1_g[l].reshape(1, D_MODEL), ln1_b[l].reshape(1, D_MODEL), wrt, br, tm_merge)

        xf = _moe(x1, gate, moe_w1[l].astype(BF16), moe_w3[l].astype(BF16), moe_w2[l].astype(BF16),
                  ln2_g[l].reshape(1, D_MODEL), ln2_b[l].reshape(1, D_MODEL), tm_moe)

    return xf.reshape(batch, seq, D_MODEL)
```

```python
import functools

import jax
import jax.numpy as jnp
from jax import lax
from jax.experimental import pallas as pl
from jax.experimental.pallas import tpu as pltpu

F32 = jnp.float32
BF16 = jnp.bfloat16

D_MODEL = 1024
DEPTH = 4
DN_HEADS = 4
DN_DK = 128
DN_CHUNK = 64
DN_CHUNKS_PER_STEP = 4
DN_W = DN_HEADS * DN_DK
SB_HEADS = 8
SB_DH = 64
SB_W = SB_HEADS * SB_DH
LRU_W = 512
LRU_BLOCKS = 8
LRU_C = 8.0
CONV_W = 4
N_EXPERTS = 16
N_GROUPS = 4
EPG = N_EXPERTS // N_GROUPS
D_FF = 512
ALPHA = (2.0 * DEPTH) ** 0.25
LN_EPS = 1e-5
RMS_EPS = 1e-6

C_DN = 0
C_MG = 2048
C_SBQ = 5120
C_SBK = 5632
C_SBV = 6144
C_LX = 6656
C_LY = 7168
P_COLS = 7680
PROJ_TN = 2560
IN_OFF = {"dn": 0, "ba": 2048, "sb": 2056, "lru": 3592}

ACT = BF16
SB_UNDERFLOW = -120.0
LANES = 128
HALO = 8
VMEM_LIMIT = 48 * 1024 * 1024


def _cparams(sem):
    return pltpu.CompilerParams(dimension_semantics=sem, vmem_limit_bytes=VMEM_LIMIT)


def _dot(a, b):
    return jnp.dot(a, b, preferred_element_type=F32)


def _dot_nt(a, b):
    return lax.dot_general(a, b, (((1,), (1,)), ((), ())), preferred_element_type=F32)


def _dot_tn(a, b):
    return lax.dot_general(a, b, (((0,), (0,)), ((), ())), preferred_element_type=F32)


def _split3(a):
    hi = a.astype(BF16)
    r1 = a - hi.astype(F32)
    mid = r1.astype(BF16)
    lo = (r1 - mid.astype(F32)).astype(BF16)
    return hi, mid, lo


def _sigmoid(x):
    return 1.0 / (1.0 + jnp.exp(-x))


def _softplus(x):
    return jnp.maximum(x, 0.0) + jnp.log(1.0 + jnp.exp(-jnp.abs(x)))


def _layer_norm(h, g, b):
    mu = jnp.mean(h, axis=-1, keepdims=True)
    hc = h - mu
    var = jnp.mean(hc * hc, axis=-1, keepdims=True)
    return hc * lax.rsqrt(var + LN_EPS) * g + b


def _proj_kernel(x_ref, w_ref, wba_ref, wbat_ref, p_ref, ba_ref, bat_ref, xb_ref):
    @pl.when(pl.program_id(1) == 0)
    def _():
        xb = x_ref[...].astype(BF16)
        xb_ref[...] = xb
        ba_ref[...] = _dot(xb, wba_ref[...])
        bat_ref[...] = _dot_nt(wbat_ref[...], xb)

    p_ref[...] = _dot(xb_ref[...], w_ref[...]).astype(p_ref.dtype)


def _proj(x, w_all, w_ba, w_bat, tm, tn):
    n = x.shape[0]
    return pl.pallas_call(
        _proj_kernel,
        grid=(n // tm, P_COLS // tn),
        in_specs=[
            pl.BlockSpec((tm, D_MODEL), lambda i, j: (i, 0)),
            pl.BlockSpec((D_MODEL, tn), lambda i, j: (0, j)),
            pl.BlockSpec((D_MODEL, LANES), lambda i, j: (0, 0)),
            pl.BlockSpec((16, D_MODEL), lambda i, j: (0, 0)),
        ],
        out_specs=[
            pl.BlockSpec((tm, tn), lambda i, j: (i, j)),
            pl.BlockSpec((tm, LANES), lambda i, j: (i, 0)),
            pl.BlockSpec((16, tm), lambda i, j: (0, i)),
        ],
        out_shape=[
            jax.ShapeDtypeStruct((n, P_COLS), ACT),
            jax.ShapeDtypeStruct((n, LANES), F32),
            jax.ShapeDtypeStruct((16, n), F32),
        ],
        scratch_shapes=[pltpu.VMEM((tm, D_MODEL), BF16)],
        compiler_params=_cparams(("parallel", "arbitrary")),
        name="proj",
    )(x, w_all, w_ba, w_bat)


def _conv_from_pad(xpad_ref, cw, r, rows):
    acc = cw[CONV_W - 1:CONV_W, :] * xpad_ref[r + HALO:r + HALO + rows, :]
    for j in range(CONV_W - 1):
        off = r + HALO - (CONV_W - 1) + j
        acc = acc + cw[j:j + 1, :] * xpad_ref[off:off + rows, :]
    return acc


def _load_halo_block(xpad_ref, blk, tb, first):
    @pl.when(first)
    def _():
        xpad_ref[0:HALO, :] = jnp.zeros((HALO, xpad_ref.shape[1]), F32)

    @pl.when(jnp.logical_not(first))
    def _():
        xpad_ref[0:HALO, :] = xpad_ref[tb:tb + HALO, :]

    xpad_ref[HALO:tb + HALO, :] = blk


def _dn_kernel(pd_ref, ba_ref, bat_ref, cw_ref, prow_ref, pcol_ref, nw_ref, ya_ref,
               xpad_ref, qkv_ref, s_ref, gcc_ref, gct_ref, u_ref, w_ref, qe_ref, kdt_ref, qk_ref, *, tb):
    C = DN_CHUNK
    nc = tb // C
    first = pl.program_id(1) == 0

    @pl.when(first)
    def _():
        s_ref[...] = jnp.zeros_like(s_ref)

    _load_halo_block(xpad_ref, pd_ref[:, 0:3 * DN_W].astype(F32), tb, first)

    cw = cw_ref[...]
    for r in range(0, tb, C):
        y = _conv_from_pad(xpad_ref, cw, r, C)
        y = y * _sigmoid(y)
        for h in range(2 * DN_HEADS):
            seg = y[:, h * DN_DK:(h + 1) * DN_DK]
            nrm = lax.rsqrt(jnp.sum(seg * seg, axis=-1, keepdims=True) + RMS_EPS)
            if h < DN_HEADS:
                nrm = nrm * (DN_DK ** -0.5)
            qkv_ref[r:r + C, h * DN_DK:(h + 1) * DN_DK] = seg * nrm
        qkv_ref[r:r + C, 2 * DN_W:3 * DN_W] = y[:, 2 * DN_W:3 * DN_W]

    g_col = -jnp.exp(prow_ref[0:1, :]) * _softplus(ba_ref[...] + prow_ref[1:2, :])
    g_row = -jnp.exp(pcol_ref[:, 0:1]) * _softplus(bat_ref[...] + pcol_ref[:, 1:2])
    ri = lax.broadcasted_iota(jnp.int32, (tb, tb), 0)
    ci = lax.broadcasted_iota(jnp.int32, (tb, tb), 1)
    same = (ri // C) == (ci // C)
    low = jnp.where(same & (ri >= ci), 1.0, 0.0).astype(BF16)
    upp = jnp.where(same & (ri <= ci), 1.0, 0.0).astype(BF16)
    c0, c1, c2 = _split3(g_col)
    gcc_ref[...] = _dot(low, c0) + _dot(low, c1) + _dot(low, c2)
    r0_, r1_, r2_ = _split3(g_row)
    gct = _dot(r0_, upp) + _dot(r1_, upp) + _dot(r2_, upp)
    for c in range(nc):
        gct_ref[c] = gct[:, c * C:(c + 1) * C]

    rr = lax.broadcasted_iota(jnp.int32, (C, C), 0)
    cc = lax.broadcasted_iota(jnp.int32, (C, C), 1)
    incl = rr >= cc
    strict = rr > cc
    eye = jnp.where(rr == cc, 1.0, 0.0).astype(F32)
    nw = nw_ref[...]

    H = range(DN_HEADS)

    def cols(h, base=0):
        return slice(base + h * DN_DK, base + (h + 1) * DN_DK)

    def phase_a(cg, carry):
        cs = [cg * DN_CHUNKS_PER_STEP + j for j in range(DN_CHUNKS_PER_STEP)]
        rws = [pl.ds(pl.multiple_of(c * C, C), C) for c in cs]
        bsig = [_sigmoid(ba_ref[rows, :]) for rows in rws]
        gcc_c = [gcc_ref[rows, :] for rows in rws]
        gct_c = [gct_ref[c] for c in cs]
        W = [(j, h) for j in range(DN_CHUNKS_PER_STEP) for h in H]
        L = range(len(W))
        q = [qkv_ref[rws[j], cols(h)] for j, h in W]
        k = [qkv_ref[rws[j], cols(h, DN_W)] for j, h in W]
        v = [qkv_ref[rws[j], cols(h, 2 * DN_W)] for j, h in W]
        beta = [bsig[j][:, h:h + 1] for j, h in W]
        gc = [gcc_c[j][:, 4 + h:5 + h] for j, h in W]
        gr = [gct_c[j][4 + h:5 + h, :] for j, h in W]
        decay = [jnp.where(incl, jnp.exp(jnp.where(incl, gc[i] - gr[i], 0.0)), 0.0) for i in L]
        kb = [k[i] * beta[i] for i in L]
        kbf = [k[i].astype(BF16) for i in L]
        kk = [_dot_nt(kb[i].astype(BF16), kbf[i]) for i in L]
        pw = [jnp.where(strict, -(kk[i] * decay[i]), 0.0) for i in L]
        t_mat = [eye + pw[i] for i in L]
        pb = [pw[i].astype(BF16) for i in L]
        for _ in range(5):
            pw = [_dot(pb[i], pb[i]) for i in L]
            pb = [pw[i].astype(BF16) for i in L]
            t_mat = [t_mat[i] + _dot(t_mat[i].astype(BF16), pb[i]) for i in L]
        eg = [jnp.exp(gc[i]) for i in L]
        uw = [_dot(t_mat[i].astype(BF16),
                   jnp.concatenate([v[i] * beta[i], kb[i] * eg[i]], axis=1).astype(BF16)) for i in L]
        qk = [_dot_nt(q[i].astype(BF16), kbf[i]) * decay[i] for i in L]
        for i, (j, h) in enumerate(W):
            u_ref[rws[j], cols(h)] = uw[i][:, 0:DN_DK]
            w_ref[rws[j], cols(h)] = uw[i][:, DN_DK:2 * DN_DK].astype(BF16)
            qe_ref[rws[j], cols(h)] = (q[i] * eg[i]).astype(BF16)
            kd = k[i] * jnp.exp(gr[i][:, C - 1:C] - gc[i])
            kdt_ref[cs[j], h] = kd.T.astype(BF16)
            qk_ref[cs[j], h] = qk[i].astype(BF16)
        return carry

    lax.fori_loop(0, nc // DN_CHUNKS_PER_STEP, phase_a, 0)

    def phase_b(c, carry):
        r0 = pl.multiple_of(c * C, C)
        rows = pl.ds(r0, C)
        gct_c = gct_ref[c]
        s_old = [s_ref[h] for h in H]
        sbf = [s_old[h].astype(BF16) for h in H]
        v_new = [u_ref[rows, cols(h)] - _dot(w_ref[rows, cols(h)], sbf[h]) for h in H]
        vnb = [v_new[h].astype(BF16) for h in H]
        o = [_dot(qe_ref[rows, cols(h)], sbf[h]) + _dot(qk_ref[c, h], vnb[h]) for h in H]
        for h in H:
            s_ref[h] = s_old[h] * jnp.exp(gct_c[4 + h:5 + h, C - 1:C]) + _dot(kdt_ref[c, h], vnb[h])
        for h in H:
            z = pd_ref[rows, cols(h, 3 * DN_W)].astype(F32)
            n = o[h] * lax.rsqrt(jnp.mean(o[h] * o[h], axis=-1, keepdims=True) + RMS_EPS) * nw
            ya_ref[rows, cols(h)] = (n * (z * _sigmoid(z))).astype(ya_ref.dtype)
        return carry

    lax.fori_loop(0, nc, phase_b, 0)


def _deltanet(p, ba, bat, cw, prow, pcol, nw, batch, seq, tb):
    n = batch * seq
    nt = seq // tb
    return pl.pallas_call(
        functools.partial(_dn_kernel, tb=tb),
        grid=(batch, nt),
        in_specs=[
            pl.BlockSpec((tb, 4 * DN_W), lambda b, t: (b * nt + t, C_DN // (4 * DN_W))),
            pl.BlockSpec((tb, LANES), lambda b, t: (b * nt + t, 0)),
            pl.BlockSpec((16, tb), lambda b, t: (0, b * nt + t)),
            pl.BlockSpec((CONV_W, 3 * DN_W), lambda b, t: (0, 0)),
            pl.BlockSpec((8, LANES), lambda b, t: (0, 0)),
            pl.BlockSpec((16, LANES), lambda b, t: (0, 0)),
            pl.BlockSpec((1, DN_DK), lambda b, t: (0, 0)),
        ],
        out_specs=pl.BlockSpec((tb, DN_W), lambda b, t: (b * nt + t, 0)),
        out_shape=jax.ShapeDtypeStruct((n, DN_W), ACT),
        scratch_shapes=[
            pltpu.VMEM((tb + HALO, 3 * DN_W), F32),
            pltpu.VMEM((tb, 3 * DN_W), F32),
            pltpu.VMEM((DN_HEADS, DN_DK, DN_DK), F32),
            pltpu.VMEM((tb, LANES), F32),
            pltpu.VMEM((tb // DN_CHUNK, 16, DN_CHUNK), F32),
            pltpu.VMEM((tb, DN_W), F32),
            pltpu.VMEM((tb, DN_W), BF16),
            pltpu.VMEM((tb, DN_W), BF16),
            pltpu.VMEM((tb // DN_CHUNK, DN_HEADS, DN_DK, DN_CHUNK), BF16),
            pltpu.VMEM((tb // DN_CHUNK, DN_HEADS, DN_CHUNK, DN_CHUNK), BF16),
        ],
        compiler_params=_cparams(("parallel", "arbitrary")),
        name="deltanet",
    )(p, ba, bat, cw, prow, pcol, nw)


def _sb_kernel(q_ref, k_ref, v_ref, o_ref, qh_ref, car_ref, acc_ref, *, tq):
    qi = pl.program_id(2)
    lane = lax.broadcasted_iota(jnp.int32, (tq, LANES), 1)
    q = q_ref[...].astype(F32) * (SB_DH ** -0.5)
    ri = lax.broadcasted_iota(jnp.int32, (tq, tq), 0)
    ci = lax.broadcasted_iota(jnp.int32, (tq, tq), 1)
    past = ci < ri
    suf1 = jnp.where(ri > ci, 1.0, 0.0).astype(BF16)
    suf = jnp.concatenate([suf1, suf1], axis=0)
    hms = [lane < SB_DH, lane >= SB_DH]
    HH = range(2)
    for h in HH:
        qh_ref[h] = jnp.where(hms[h], q, 0.0).astype(BF16)
    car_ref[...] = jnp.zeros_like(car_ref)
    acc_ref[...] = jnp.zeros_like(acc_ref)

    def blk(kb, masked):
        k0 = pl.multiple_of(kb * tq, tq)
        kk = k_ref[pl.ds(k0, tq), :].astype(BF16)
        vv = v_ref[pl.ds(k0, tq), :].astype(BF16)
        z = [_dot_nt(qh_ref[h], kk) for h in HH]
        sp = [_softplus(z[h]) for h in HH]
        ls = [-sp[h] for h in HH]
        if masked:
            ls = [jnp.where(past, ls[h], 0.0) for h in HH]
        hi = [ls[h].astype(BF16) for h in HH]
        lo = [(ls[h] - hi[h].astype(F32)).astype(BF16) for h in HH]
        after = [_dot(jnp.concatenate([hi[h], lo[h]], axis=1), suf) for h in HH]
        att = [jnp.exp((z[h] - sp[h]) + after[h] + car_ref[h]) for h in HH]
        if masked:
            att = [jnp.where(past, att[h], 0.0) for h in HH]
        for h in HH:
            acc_ref[h] += _dot(att[h].astype(BF16), vv)
            car_ref[h] += jnp.sum(ls[h], axis=1, keepdims=True)
        return jnp.max(jnp.maximum(car_ref[0], car_ref[1]))

    def cond(st):
        kb, top = st
        return (kb >= 0) & (top > SB_UNDERFLOW)

    def body(st):
        kb, _ = st
        return kb - 1, blk(kb, False)

    lax.while_loop(cond, body, (qi - 1, blk(qi, True)))
    o_ref[...] = jnp.where(hms[0], acc_ref[0], acc_ref[1]).astype(o_ref.dtype)


def _stickbreak(p, batch, seq, tq):
    n = batch * seq
    nq = seq // tq
    hp = SB_W // LANES
    return pl.pallas_call(
        functools.partial(_sb_kernel, tq=tq),
        grid=(batch, hp, nq),
        in_specs=[
            pl.BlockSpec((tq, LANES), lambda b, h, i: (b * nq + i, C_SBQ // LANES + h)),
            pl.BlockSpec((seq, LANES), lambda b, h, i: (b, C_SBK // LANES + h)),
            pl.BlockSpec((seq, LANES), lambda b, h, i: (b, C_SBV // LANES + h)),
        ],
        out_specs=pl.BlockSpec((tq, LANES), lambda b, h, i: (b * nq + i, h)),
        out_shape=jax.ShapeDtypeStruct((n, SB_W), ACT),
        scratch_shapes=[
            pltpu.VMEM((2, tq, LANES), BF16),
            pltpu.VMEM((2, tq, 1), F32),
            pltpu.VMEM((2, tq, LANES), F32),
        ],
        compiler_params=_cparams(("parallel", "parallel", "arbitrary")),
        name="stickbreak",
    )(p, p, p)


def _lru_kernel(px_ref, py_ref, cw_ref, cb_ref, wa_ref, ba_ref, wx_ref, bx_ref, lam_ref, yc_ref,
                xpad_ref, a_ref, h_ref, car_ref, *, tb):
    first = pl.program_id(1) == 0

    @pl.when(first)
    def _():
        car_ref[...] = jnp.zeros_like(car_ref)

    _load_halo_block(xpad_ref, px_ref[...].astype(F32), tb, first)
    xc = _conv_from_pad(xpad_ref, cw_ref[...], 0, tb) + cb_ref[...]
    xcb = xc.astype(BF16)
    r = _sigmoid(_dot(xcb, wa_ref[...]) + ba_ref[...])
    i = _sigmoid(_dot(xcb, wx_ref[...]) + bx_ref[...])
    log_a = -LRU_C * r * _softplus(-lam_ref[...])
    a_ref[...] = jnp.exp(log_a)
    h_ref[...] = jnp.sqrt(1.0 - jnp.exp(2.0 * log_a)) * (i * xc)

    row = lax.broadcasted_iota(jnp.int32, (8, LRU_W), 0)

    def tile(j, h):
        r0 = pl.multiple_of(j * 8, 8)
        a = a_ref[pl.ds(r0, 8), :]
        b = h_ref[pl.ds(r0, 8), :]
        for s in (1, 2, 4):
            m = row >= s
            b = jnp.where(m, a * pltpu.roll(b, s, axis=0) + b, b)
            a = jnp.where(m, a * pltpu.roll(a, s, axis=0), a)
        hh = a * h + b
        h_ref[pl.ds(r0, 8), :] = hh
        return hh[7:8, :]

    car_ref[0:1, :] = lax.fori_loop(0, tb // 8, tile, car_ref[0:1, :])
    y = py_ref[...].astype(F32)
    gelu = 0.5 * y * (1.0 + jnp.tanh(0.7978845608028654 * (y + 0.044715 * (y * y * y))))
    yc_ref[...] = (gelu * h_ref[...]).astype(yc_ref.dtype)


def _rglru(p, cw, cb, wa, ba, wx, bx, lam, batch, seq, tb):
    n = batch * seq
    nt = seq // tb
    vec = pl.BlockSpec((1, LRU_W), lambda b, t: (0, 0))
    mat = pl.BlockSpec((LRU_W, LRU_W), lambda b, t: (0, 0))
    return pl.pallas_call(
        functools.partial(_lru_kernel, tb=tb),
        grid=(batch, nt),
        in_specs=[
            pl.BlockSpec((tb, LRU_W), lambda b, t: (b * nt + t, C_LX // LRU_W)),
            pl.BlockSpec((tb, LRU_W), lambda b, t: (b * nt + t, C_LY // LRU_W)),
            pl.BlockSpec((CONV_W, LRU_W), lambda b, t: (0, 0)),
            vec, mat, vec, mat, vec, vec,
        ],
        out_specs=pl.BlockSpec((tb, LRU_W), lambda b, t: (b * nt + t, 0)),
        out_shape=jax.ShapeDtypeStruct((n, LRU_W), ACT),
        scratch_shapes=[
            pltpu.VMEM((tb + HALO, LRU_W), F32),
            pltpu.VMEM((tb, LRU_W), F32),
            pltpu.VMEM((tb, LRU_W), F32),
            pltpu.VMEM((8, LRU_W), F32),
        ],
        compiler_params=_cparams(("parallel", "arbitrary")),
        name="rglru",
    )(p, p, cw, cb, wa, ba, wx, bx, lam)


def _top2_sum(a, b, c, d):
    m1, n1 = jnp.maximum(a, b), jnp.minimum(a, b)
    m2, n2 = jnp.maximum(c, d), jnp.minimum(c, d)
    return jnp.maximum(m1, m2) + jnp.maximum(jnp.minimum(m1, m2), jnp.maximum(n1, n2))


def _merge_kernel(x_ref, g0_ref, g1_ref, g2_ref, ya_ref, yb_ref, yc_ref, wa_ref, wb_ref, wc_ref, wo_ref,
                  bm_ref, lng_ref, lnb_ref, wrt_ref, br_ref, x1_ref, gate_ref, gt_ref):
    merged = (_sigmoid(g0_ref[...].astype(F32) + bm_ref[0:1, :]) * _dot(ya_ref[...].astype(BF16), wa_ref[...])
              + _sigmoid(g1_ref[...].astype(F32) + bm_ref[1:2, :]) * _dot(yb_ref[...].astype(BF16), wb_ref[...])
              + _sigmoid(g2_ref[...].astype(F32) + bm_ref[2:3, :]) * _dot(yc_ref[...].astype(BF16), wc_ref[...]))
    mix = _dot(merged.astype(BF16), wo_ref[...])
    x1 = _layer_norm(ALPHA * x_ref[...] + mix, lng_ref[...], lnb_ref[...])
    x1_ref[...] = x1

    lt = _dot_nt(wrt_ref[...], x1.astype(BF16))
    ex = jnp.exp(lt - jnp.max(lt, axis=0, keepdims=True))
    score = ex / jnp.sum(ex, axis=0, keepdims=True)
    sel = score + br_ref[:, 0:1]
    sc = [score[e:e + 1, :] for e in range(N_EXPERTS)]
    sl = [sel[e:e + 1, :] for e in range(N_EXPERTS)]

    best = _top2_sum(*sl[0:EPG])
    gidx = jnp.zeros_like(best, dtype=jnp.int32)
    for g in range(1, N_GROUPS):
        gs = _top2_sum(*sl[g * EPG:(g + 1) * EPG])
        better = gs > best
        gidx = jnp.where(better, g, gidx)
        best = jnp.where(better, gs, best)

    def pick(vals, j):
        out = vals[j]
        for g in range(1, N_GROUPS):
            out = jnp.where(gidx == g, vals[g * EPG + j], out)
        return out

    vs = [pick(sl, j) for j in range(EPG)]
    ws = [pick(sc, j) for j in range(EPG)]
    b1, i1, w1 = vs[0], jnp.zeros_like(gidx), ws[0]
    for j in range(1, EPG):
        gt = vs[j] > b1
        i1 = jnp.where(gt, j, i1)
        w1 = jnp.where(gt, ws[j], w1)
        b1 = jnp.where(gt, vs[j], b1)
    b2 = jnp.full_like(b1, -jnp.inf)
    i2 = jnp.zeros_like(gidx)
    w2 = jnp.zeros_like(w1)
    for j in range(EPG):
        gt = (i1 != j) & (vs[j] > b2)
        i2 = jnp.where(gt, j, i2)
        w2 = jnp.where(gt, ws[j], w2)
        b2 = jnp.where(gt, vs[j], b2)
    den = w1 + w2
    w1n = w1 / den
    w2n = w2 / den

    gt_ref[...] = jnp.zeros_like(gt_ref)
    for g in range(N_GROUPS):
        ing = gidx == g
        for j in range(EPG):
            e = g * EPG + j
            gt_ref[e:e + 1, :] = (jnp.where(ing & (i1 == j), w1n, 0.0) + jnp.where(ing & (i2 == j), w2n, 0.0))
    gate_ref[...] = gt_ref[...].T


def _merge(x, p, ya, yb, yc, wa, wb, wc, wo, bm, lng, lnb, wrt, br, tm):
    n = x.shape[0]
    row = lambda w: pl.BlockSpec((tm, w), lambda i: (i, 0))
    full = lambda a: pl.BlockSpec(a.shape, lambda i: (0, 0))
    mg = C_MG // D_MODEL
    return pl.pallas_call(
        _merge_kernel,
        grid=(n // tm,),
        in_specs=[
            row(D_MODEL),
            pl.BlockSpec((tm, D_MODEL), lambda i: (i, mg)),
            pl.BlockSpec((tm, D_MODEL), lambda i: (i, mg + 1)),
            pl.BlockSpec((tm, D_MODEL), lambda i: (i, mg + 2)),
            row(DN_W), row(SB_W), row(LRU_W),
            full(wa), full(wb), full(wc), full(wo), full(bm), full(lng), full(lnb), full(wrt), full(br),
        ],
        out_specs=[row(D_MODEL), row(LANES)],
        out_shape=[jax.ShapeDtypeStruct((n, D_MODEL), F32), jax.ShapeDtypeStruct((n, LANES), F32)],
        scratch_shapes=[pltpu.VMEM((LANES, tm), F32)],
        compiler_params=_cparams(("parallel",)),
        name="merge",
    )(x, p, p, p, ya, yb, yc, wa, wb, wc, wo, bm, lng, lnb, wrt, br)


def _moe_kernel(x_ref, gate_ref, w1_ref, w3_ref, w2_ref, lng_ref, lnb_ref, o_ref, xb_ref, acc_ref):
    e = pl.program_id(1)

    @pl.when(e == 0)
    def _():
        xb_ref[...] = x_ref[...].astype(BF16)
        acc_ref[...] = jnp.zeros_like(acc_ref)

    xb = xb_ref[...]
    h1 = _dot(xb, w1_ref[0])
    h3 = _dot(xb, w3_ref[0])
    lane = lax.broadcasted_iota(jnp.int32, gate_ref.shape, 1)
    ge = jnp.sum(jnp.where(lane == e, gate_ref[...], 0.0), axis=1, keepdims=True)
    hh = (h1 * _sigmoid(h1)) * h3 * ge
    acc_ref[...] += _dot(hh.astype(BF16), w2_ref[0])

    @pl.when(e == N_EXPERTS - 1)
    def _():
        o_ref[...] = _layer_norm(ALPHA * x_ref[...] + acc_ref[...], lng_ref[...], lnb_ref[...])


def _moe(x1, gate, w1, w3, w2, lng, lnb, tm):
    n = x1.shape[0]
    vec = pl.BlockSpec((1, D_MODEL), lambda i, e: (0, 0))
    return pl.pallas_call(
        _moe_kernel,
        grid=(n // tm, N_EXPERTS),
        in_specs=[
            pl.BlockSpec((tm, D_MODEL), lambda i, e: (i, 0)),
            pl.BlockSpec((tm, LANES), lambda i, e: (i, 0)),
            pl.BlockSpec((1, D_MODEL, D_FF), lambda i, e: (e, 0, 0)),
            pl.BlockSpec((1, D_MODEL, D_FF), lambda i, e: (e, 0, 0)),
            pl.BlockSpec((1, D_FF, D_MODEL), lambda i, e: (e, 0, 0)),
            vec, vec,
        ],
        out_specs=pl.BlockSpec((tm, D_MODEL), lambda i, e: (i, 0)),
        out_shape=jax.ShapeDtypeStruct((n, D_MODEL), F32),
        scratch_shapes=[pltpu.VMEM((tm, D_MODEL), BF16), pltpu.VMEM((tm, D_MODEL), F32)],
        compiler_params=_cparams(("parallel", "arbitrary")),
        name="moe",
    )(x1, gate, w1, w3, w2, lng, lnb)


def _pad_rows(a, rows):
    return jnp.pad(a, ((0, rows - a.shape[0]), (0, 0)))


def _block_diag(w):
    nb, c, _ = w.shape
    eye = jnp.eye(nb, dtype=w.dtype)
    return jnp.einsum("ncd,nm->ncmd", w, eye).reshape(nb * c, nb * c)


def kernel(x, w_in, dn_conv_w, dn_a_log, dn_dt_bias, dn_norm_w, lru_conv_w, lru_conv_b, lru_w_a, lru_b_a,
           lru_w_x, lru_b_x, lru_lambda, w_up_dn, w_up_sb, w_up_lru, w_merge, b_merge, w_out, ln1_g, ln1_b,
           w_router, b_router, moe_w1, moe_w3, moe_w2, ln2_g, ln2_b):
    batch, seq, _ = x.shape
    n = batch * seq
    tb = min(512, seq)
    tq = min(256, seq)
    tm_proj = min(1024, n)
    tm_merge = min(256, n)
    tm_moe = min(1024, n)
    xf = x.reshape(n, D_MODEL)

    wrt = _pad_rows(w_router.T, 16).astype(BF16)
    br = jnp.pad(b_router.reshape(N_EXPERTS, 1), ((0, 0), (0, LANES - 1)))

    for l in range(DEPTH):
        wi = w_in[l]
        w_all = jnp.concatenate(
            [wi[:, IN_OFF["dn"]:IN_OFF["ba"]], w_merge[l], wi[:, IN_OFF["sb"]:IN_OFF["lru"]], wi[:, IN_OFF["lru"]:]],
            axis=1).astype(BF16)
        w_ba8 = wi[:, IN_OFF["ba"]:IN_OFF["sb"]]
        w_ba = jnp.pad(w_ba8, ((0, 0), (0, LANES - 8))).astype(BF16)
        w_bat = _pad_rows(w_ba8.T, 16).astype(BF16)
        p, ba, bat = _proj(xf, w_all, w_ba, w_bat, tm_proj, PROJ_TN)

        pvec = jnp.stack([dn_a_log[l], dn_dt_bias[l]])
        prow = jnp.pad(pvec, ((0, 6), (4, LANES - 8)))
        pcol = jnp.pad(pvec.T, ((4, 8), (0, LANES - 2)))
        ya = _deltanet(p, ba, bat, dn_conv_w[l], prow, pcol, dn_norm_w[l].reshape(1, DN_DK), batch, seq, tb)

        yb = _stickbreak(p, batch, seq, tq)

        yc = _rglru(p, lru_conv_w[l], lru_conv_b[l].reshape(1, LRU_W),
                    _block_diag(lru_w_a[l]).astype(BF16), lru_b_a[l].reshape(1, LRU_W),
                    _block_diag(lru_w_x[l]).astype(BF16), lru_b_x[l].reshape(1, LRU_W),
                    lru_lambda[l].reshape(1, LRU_W), batch, seq, tb)

        x1, gate = _merge(xf, p, ya, yb, yc,
                          w_up_dn[l].astype(BF16), w_up_sb[l].astype(BF16), w_up_lru[l].astype(BF16),
                          w_out[l].astype(BF16), _pad_rows(b_merge[l].reshape(3, D_MODEL), 8),
                          ln1_g[l].reshape(1, D_MODEL), ln1_b[l].reshape(1, D_MODEL), wrt, br, tm_merge)

        xf = _moe(x1, gate, moe_w1[l].astype(BF16), moe_w3[l].astype(BF16), moe_w2[l].astype(BF16),
                  ln2_g[l].reshape(1, D_MODEL), ln2_b[l].reshape(1, D_MODEL), tm_moe)

    return xf.reshape(batch, seq, D_MODEL)
```

```python
import functools

import jax
import jax.numpy as jnp
from jax import lax
from jax.experimental import pallas as pl
from jax.experimental.pallas import tpu as pltpu

F32 = jnp.float32
BF16 = jnp.bfloat16

D_MODEL = 1024
DEPTH = 4
DN_HEADS = 4
DN_DK = 128
DN_CHUNK = 64
DN_CHUNKS_PER_STEP = 4
DN_W = DN_HEADS * DN_DK
SB_HEADS = 8
SB_DH = 64
SB_W = SB_HEADS * SB_DH
SB_BLOCK_W = 256
LRU_W = 512
LRU_BLOCKS = 8
LRU_C = 8.0
CONV_W = 4
N_EXPERTS = 16
N_GROUPS = 4
EPG = N_EXPERTS // N_GROUPS
D_FF = 512
ALPHA = (2.0 * DEPTH) ** 0.25
LN_EPS = 1e-5
RMS_EPS = 1e-6

C_DN = 0
C_MG = 2048
C_SBQ = 5120
C_SBK = 5632
C_SBV = 6144
C_LX = 6656
C_LY = 7168
P_COLS = 7680
PROJ_TN = 2560
IN_OFF = {"dn": 0, "ba": 2048, "sb": 2056, "lru": 3592}

GATE_ROWS = 24
MOE_SUBTILE = 512
MOE_CAP_NUM, MOE_CAP_DEN = 3, 8
ACT = BF16
SB_UNDERFLOW = -120.0
LANES = 128
HALO = 8
VMEM_LIMIT = 48 * 1024 * 1024


def _cparams(sem):
    return pltpu.CompilerParams(dimension_semantics=sem, vmem_limit_bytes=VMEM_LIMIT)


def _dot(a, b):
    return jnp.dot(a, b, preferred_element_type=F32)


def _dot_nt(a, b):
    return lax.dot_general(a, b, (((1,), (1,)), ((), ())), preferred_element_type=F32)


def _dot_tn(a, b):
    return lax.dot_general(a, b, (((0,), (0,)), ((), ())), preferred_element_type=F32)


def _split3(a):
    hi = a.astype(BF16)
    r1 = a - hi.astype(F32)
    mid = r1.astype(BF16)
    lo = (r1 - mid.astype(F32)).astype(BF16)
    return hi, mid, lo


def _sigmoid(x):
    return 1.0 / (1.0 + jnp.exp(-x))


def _softplus(x):
    return jnp.maximum(x, 0.0) + jnp.log(1.0 + jnp.exp(-jnp.abs(x)))


def _layer_norm(h, g, b):
    mu = jnp.mean(h, axis=-1, keepdims=True)
    hc = h - mu
    var = jnp.mean(hc * hc, axis=-1, keepdims=True)
    return hc * lax.rsqrt(var + LN_EPS) * g + b


def _proj_kernel(x_ref, w_ref, wba_ref, wbat_ref, p_ref, ba_ref, bat_ref, xb_ref):
    @pl.when(pl.program_id(1) == 0)
    def _():
        xb = x_ref[...].astype(BF16)
        xb_ref[...] = xb
        ba_ref[...] = _dot(xb, wba_ref[...])
        bat_ref[...] = _dot_nt(wbat_ref[...], xb)

    p_ref[...] = _dot(xb_ref[...], w_ref[...]).astype(p_ref.dtype)


def _proj(x, w_all, w_ba, w_bat, tm, tn):
    n = x.shape[0]
    return pl.pallas_call(
        _proj_kernel,
        grid=(n // tm, P_COLS // tn),
        in_specs=[
            pl.BlockSpec((tm, D_MODEL), lambda i, j: (i, 0)),
            pl.BlockSpec((D_MODEL, tn), lambda i, j: (0, j)),
            pl.BlockSpec((D_MODEL, LANES), lambda i, j: (0, 0)),
            pl.BlockSpec((16, D_MODEL), lambda i, j: (0, 0)),
        ],
        out_specs=[
            pl.BlockSpec((tm, tn), lambda i, j: (i, j)),
            pl.BlockSpec((tm, LANES), lambda i, j: (i, 0)),
            pl.BlockSpec((16, tm), lambda i, j: (0, i)),
        ],
        out_shape=[
            jax.ShapeDtypeStruct((n, P_COLS), ACT),
            jax.ShapeDtypeStruct((n, LANES), F32),
            jax.ShapeDtypeStruct((16, n), F32),
        ],
        scratch_shapes=[pltpu.VMEM((tm, D_MODEL), BF16)],
        compiler_params=_cparams(("parallel", "arbitrary")),
        name="proj",
    )(x, w_all, w_ba, w_bat)


def _conv_from_pad(xpad_ref, cw, r, rows):
    acc = cw[CONV_W - 1:CONV_W, :] * xpad_ref[r + HALO:r + HALO + rows, :]
    for j in range(CONV_W - 1):
        off = r + HALO - (CONV_W - 1) + j
        acc = acc + cw[j:j + 1, :] * xpad_ref[off:off + rows, :]
    return acc


def _load_halo_block(xpad_ref, blk, tb, first):
    @pl.when(first)
    def _():
        xpad_ref[0:HALO, :] = jnp.zeros((HALO, xpad_ref.shape[1]), F32)

    @pl.when(jnp.logical_not(first))
    def _():
        xpad_ref[0:HALO, :] = xpad_ref[tb:tb + HALO, :]

    xpad_ref[HALO:tb + HALO, :] = blk


def _dn_kernel(pd_ref, ba_ref, bat_ref, cw_ref, prow_ref, pcol_ref, nw_ref, ya_ref,
               xpad_ref, qkv_ref, s_ref, gcc_ref, gct_ref, u_ref, w_ref, qe_ref, kdt_ref, qk_ref, *, tb):
    C = DN_CHUNK
    nc = tb // C
    first = pl.program_id(1) == 0

    @pl.when(first)
    def _():
        s_ref[...] = jnp.zeros_like(s_ref)

    _load_halo_block(xpad_ref, pd_ref[:, 0:3 * DN_W].astype(F32), tb, first)

    cw = cw_ref[...]
    for r in range(0, tb, C):
        y = _conv_from_pad(xpad_ref, cw, r, C)
        y = y * _sigmoid(y)
        for h in range(2 * DN_HEADS):
            seg = y[:, h * DN_DK:(h + 1) * DN_DK]
            nrm = lax.rsqrt(jnp.sum(seg * seg, axis=-1, keepdims=True) + RMS_EPS)
            if h < DN_HEADS:
                nrm = nrm * (DN_DK ** -0.5)
            qkv_ref[r:r + C, h * DN_DK:(h + 1) * DN_DK] = seg * nrm
        qkv_ref[r:r + C, 2 * DN_W:3 * DN_W] = y[:, 2 * DN_W:3 * DN_W]

    g_col = -jnp.exp(prow_ref[0:1, :]) * _softplus(ba_ref[...] + prow_ref[1:2, :])
    g_row = -jnp.exp(pcol_ref[:, 0:1]) * _softplus(bat_ref[...] + pcol_ref[:, 1:2])
    ri = lax.broadcasted_iota(jnp.int32, (tb, tb), 0)
    ci = lax.broadcasted_iota(jnp.int32, (tb, tb), 1)
    same = (ri // C) == (ci // C)
    low = jnp.where(same & (ri >= ci), 1.0, 0.0).astype(BF16)
    upp = jnp.where(same & (ri <= ci), 1.0, 0.0).astype(BF16)
    c0, c1, c2 = _split3(g_col)
    gcc_ref[...] = _dot(low, c0) + _dot(low, c1) + _dot(low, c2)
    r0_, r1_, r2_ = _split3(g_row)
    gct = _dot(r0_, upp) + _dot(r1_, upp) + _dot(r2_, upp)
    for c in range(nc):
        gct_ref[c] = gct[:, c * C:(c + 1) * C]

    rr = lax.broadcasted_iota(jnp.int32, (C, C), 0)
    cc = lax.broadcasted_iota(jnp.int32, (C, C), 1)
    incl = rr >= cc
    strict = rr > cc
    eye = jnp.where(rr == cc, 1.0, 0.0).astype(F32)
    nw = nw_ref[...]

    H = range(DN_HEADS)

    def cols(h, base=0):
        return slice(base + h * DN_DK, base + (h + 1) * DN_DK)

    def phase_a(cg, carry):
        cs = [cg * DN_CHUNKS_PER_STEP + j for j in range(DN_CHUNKS_PER_STEP)]
        rws = [pl.ds(pl.multiple_of(c * C, C), C) for c in cs]
        bsig = [_sigmoid(ba_ref[rows, :]) for rows in rws]
        gcc_c = [gcc_ref[rows, :] for rows in rws]
        gct_c = [gct_ref[c] for c in cs]
        W = [(j, h) for j in range(DN_CHUNKS_PER_STEP) for h in H]
        L = range(len(W))
        q = [qkv_ref[rws[j], cols(h)] for j, h in W]
        k = [qkv_ref[rws[j], cols(h, DN_W)] for j, h in W]
        v = [qkv_ref[rws[j], cols(h, 2 * DN_W)] for j, h in W]
        beta = [bsig[j][:, h:h + 1] for j, h in W]
        gc = [gcc_c[j][:, 4 + h:5 + h] for j, h in W]
        gr = [gct_c[j][4 + h:5 + h, :] for j, h in W]
        decay = [jnp.where(incl, jnp.exp(jnp.where(incl, gc[i] - gr[i], 0.0)), 0.0) for i in L]
        kb = [k[i] * beta[i] for i in L]
        kbf = [k[i].astype(BF16) for i in L]
        kk = [_dot_nt(kb[i].astype(BF16), kbf[i]) for i in L]
        pw = [jnp.where(strict, -(kk[i] * decay[i]), 0.0) for i in L]
        t_mat = [eye + pw[i] for i in L]
        pb = [pw[i].astype(BF16) for i in L]
        for _ in range(5):
            pw = [_dot(pb[i], pb[i]) for i in L]
            pb = [pw[i].astype(BF16) for i in L]
            t_mat = [t_mat[i] + _dot(t_mat[i].astype(BF16), pb[i]) for i in L]
        eg = [jnp.exp(gc[i]) for i in L]
        uw = [_dot(t_mat[i].astype(BF16),
                   jnp.concatenate([v[i] * beta[i], kb[i] * eg[i]], axis=1).astype(BF16)) for i in L]
        qk = [_dot_nt(q[i].astype(BF16), kbf[i]) * decay[i] for i in L]
        for i, (j, h) in enumerate(W):
            u_ref[rws[j], cols(h)] = uw[i][:, 0:DN_DK]
            w_ref[rws[j], cols(h)] = uw[i][:, DN_DK:2 * DN_DK].astype(BF16)
            qe_ref[rws[j], cols(h)] = (q[i] * eg[i]).astype(BF16)
            kd = k[i] * jnp.exp(gr[i][:, C - 1:C] - gc[i])
            kdt_ref[cs[j], h] = kd.T.astype(BF16)
            qk_ref[cs[j], h] = qk[i].astype(BF16)
        return carry

    lax.fori_loop(0, nc // DN_CHUNKS_PER_STEP, phase_a, 0)

    def phase_b(c, carry):
        r0 = pl.multiple_of(c * C, C)
        rows = pl.ds(r0, C)
        gct_c = gct_ref[c]
        s_old = [s_ref[h] for h in H]
        sbf = [s_old[h].astype(BF16) for h in H]
        v_new = [u_ref[rows, cols(h)] - _dot(w_ref[rows, cols(h)], sbf[h]) for h in H]
        vnb = [v_new[h].astype(BF16) for h in H]
        o = [_dot(qe_ref[rows, cols(h)], sbf[h]) + _dot(qk_ref[c, h], vnb[h]) for h in H]
        for h in H:
            s_ref[h] = s_old[h] * jnp.exp(gct_c[4 + h:5 + h, C - 1:C]) + _dot(kdt_ref[c, h], vnb[h])
        for h in H:
            z = pd_ref[rows, cols(h, 3 * DN_W)].astype(F32)
            n = o[h] * lax.rsqrt(jnp.mean(o[h] * o[h], axis=-1, keepdims=True) + RMS_EPS) * nw
            ya_ref[rows, cols(h)] = (n * (z * _sigmoid(z))).astype(ya_ref.dtype)
        return carry

    lax.fori_loop(0, nc, phase_b, 0)


def _deltanet(p, ba, bat, cw, prow, pcol, nw, batch, seq, tb):
    n = batch * seq
    nt = seq // tb
    return pl.pallas_call(
        functools.partial(_dn_kernel, tb=tb),
        grid=(batch, nt),
        in_specs=[
            pl.BlockSpec((tb, 4 * DN_W), lambda b, t: (b * nt + t, C_DN // (4 * DN_W))),
            pl.BlockSpec((tb, LANES), lambda b, t: (b * nt + t, 0)),
            pl.BlockSpec((16, tb), lambda b, t: (0, b * nt + t)),
            pl.BlockSpec((CONV_W, 3 * DN_W), lambda b, t: (0, 0)),
            pl.BlockSpec((8, LANES), lambda b, t: (0, 0)),
            pl.BlockSpec((16, LANES), lambda b, t: (0, 0)),
            pl.BlockSpec((1, DN_DK), lambda b, t: (0, 0)),
        ],
        out_specs=pl.BlockSpec((tb, DN_W), lambda b, t: (b * nt + t, 0)),
        out_shape=jax.ShapeDtypeStruct((n, DN_W), ACT),
        scratch_shapes=[
            pltpu.VMEM((tb + HALO, 3 * DN_W), F32),
            pltpu.VMEM((tb, 3 * DN_W), F32),
            pltpu.VMEM((DN_HEADS, DN_DK, DN_DK), F32),
            pltpu.VMEM((tb, LANES), F32),
            pltpu.VMEM((tb // DN_CHUNK, 16, DN_CHUNK), F32),
            pltpu.VMEM((tb, DN_W), F32),
            pltpu.VMEM((tb, DN_W), BF16),
            pltpu.VMEM((tb, DN_W), BF16),
            pltpu.VMEM((tb // DN_CHUNK, DN_HEADS, DN_DK, DN_CHUNK), BF16),
            pltpu.VMEM((tb // DN_CHUNK, DN_HEADS, DN_CHUNK, DN_CHUNK), BF16),
        ],
        compiler_params=_cparams(("parallel", "arbitrary")),
        name="deltanet",
    )(p, ba, bat, cw, prow, pcol, nw)


def _sb_kernel(q_ref, k_ref, v_ref, o_ref, qh_ref, car_ref, acc_ref, *, tq):
    qi = pl.program_id(2)
    lane = lax.broadcasted_iota(jnp.int32, (tq, SB_BLOCK_W), 1)
    q = q_ref[...].astype(F32) * (SB_DH ** -0.5)
    ri = lax.broadcasted_iota(jnp.int32, (tq, tq), 0)
    ci = lax.broadcasted_iota(jnp.int32, (tq, tq), 1)
    past = ci < ri
    suf1 = jnp.where(ri > ci, 1.0, 0.0).astype(BF16)
    suf = jnp.concatenate([suf1, suf1], axis=0)
    HH = range(SB_BLOCK_W // SB_DH)
    hms = [(lane >= h * SB_DH) & (lane < (h + 1) * SB_DH) for h in HH]
    for h in HH:
        qh_ref[h] = jnp.where(hms[h], q, 0.0).astype(BF16)
    car_ref[...] = jnp.zeros_like(car_ref)
    acc_ref[...] = jnp.zeros_like(acc_ref)

    def blk(kb, masked):
        k0 = pl.multiple_of(kb * tq, tq)
        kk = k_ref[pl.ds(k0, tq), :].astype(BF16)
        vv = v_ref[pl.ds(k0, tq), :].astype(BF16)
        z = [_dot_nt(qh_ref[h], kk) for h in HH]
        sp = [_softplus(z[h]) for h in HH]
        ls = [-sp[h] for h in HH]
        if masked:
            ls = [jnp.where(past, ls[h], 0.0) for h in HH]
        hi = [ls[h].astype(BF16) for h in HH]
        lo = [(ls[h] - hi[h].astype(F32)).astype(BF16) for h in HH]
        after = [_dot(jnp.concatenate([hi[h], lo[h]], axis=1), suf) for h in HH]
        att = [jnp.exp((z[h] - sp[h]) + after[h] + car_ref[h]) for h in HH]
        if masked:
            att = [jnp.where(past, att[h], 0.0) for h in HH]
        for h in HH:
            acc_ref[h] += _dot(att[h].astype(BF16), vv)
            car_ref[h] += jnp.sum(ls[h], axis=1, keepdims=True)
        top = car_ref[0]
        for h in HH[1:]:
            top = jnp.maximum(top, car_ref[h])
        return jnp.max(top)

    def cond(st):
        kb, top = st
        return (kb >= 0) & (top > SB_UNDERFLOW)

    def body(st):
        kb, _ = st
        return kb - 1, blk(kb, False)

    lax.while_loop(cond, body, (qi - 1, blk(qi, True)))
    out = acc_ref[0]
    for h in HH[1:]:
        out = jnp.where(hms[h], acc_ref[h], out)
    o_ref[...] = out.astype(o_ref.dtype)


def _stickbreak(p, batch, seq, tq):
    n = batch * seq
    nq = seq // tq
    bw = SB_BLOCK_W
    nh = bw // SB_DH
    return pl.pallas_call(
        functools.partial(_sb_kernel, tq=tq),
        grid=(batch, SB_W // bw, nq),
        in_specs=[
            pl.BlockSpec((tq, bw), lambda b, h, i: (b * nq + i, C_SBQ // bw + h)),
            pl.BlockSpec((seq, bw), lambda b, h, i: (b, C_SBK // bw + h)),
            pl.BlockSpec((seq, bw), lambda b, h, i: (b, C_SBV // bw + h)),
        ],
        out_specs=pl.BlockSpec((tq, bw), lambda b, h, i: (b * nq + i, h)),
        out_shape=jax.ShapeDtypeStruct((n, SB_W), ACT),
        scratch_shapes=[
            pltpu.VMEM((nh, tq, bw), BF16),
            pltpu.VMEM((nh, tq, 1), F32),
            pltpu.VMEM((nh, tq, bw), F32),
        ],
        compiler_params=_cparams(("parallel", "parallel", "arbitrary")),
        name="stickbreak",
    )(p, p, p)


def _lru_kernel(px_ref, py_ref, cw_ref, cb_ref, wa_ref, ba_ref, wx_ref, bx_ref, lam_ref, yc_ref,
                xpad_ref, a_ref, h_ref, car_ref, *, tb):
    first = pl.program_id(1) == 0

    @pl.when(first)
    def _():
        car_ref[...] = jnp.zeros_like(car_ref)

    _load_halo_block(xpad_ref, px_ref[...].astype(F32), tb, first)
    xc = _conv_from_pad(xpad_ref, cw_ref[...], 0, tb) + cb_ref[...]
    xcb = xc.astype(BF16)
    r = _sigmoid(_dot(xcb, wa_ref[...]) + ba_ref[...])
    i = _sigmoid(_dot(xcb, wx_ref[...]) + bx_ref[...])
    log_a = -LRU_C * r * _softplus(-lam_ref[...])
    a_ref[...] = jnp.exp(log_a)
    h_ref[...] = jnp.sqrt(1.0 - jnp.exp(2.0 * log_a)) * (i * xc)

    row = lax.broadcasted_iota(jnp.int32, (8, LRU_W), 0)

    def tile(j, h):
        r0 = pl.multiple_of(j * 8, 8)
        a = a_ref[pl.ds(r0, 8), :]
        b = h_ref[pl.ds(r0, 8), :]
        for s in (1, 2, 4):
            m = row >= s
            b = jnp.where(m, a * pltpu.roll(b, s, axis=0) + b, b)
            a = jnp.where(m, a * pltpu.roll(a, s, axis=0), a)
        hh = a * h + b
        h_ref[pl.ds(r0, 8), :] = hh
        return hh[7:8, :]

    car_ref[0:1, :] = lax.fori_loop(0, tb // 8, tile, car_ref[0:1, :])
    y = py_ref[...].astype(F32)
    gelu = 0.5 * y * (1.0 + jnp.tanh(0.7978845608028654 * (y + 0.044715 * (y * y * y))))
    yc_ref[...] = (gelu * h_ref[...]).astype(yc_ref.dtype)


def _rglru(p, cw, cb, wa, ba, wx, bx, lam, batch, seq, tb):
    n = batch * seq
    nt = seq // tb
    vec = pl.BlockSpec((1, LRU_W), lambda b, t: (0, 0))
    mat = pl.BlockSpec((LRU_W, LRU_W), lambda b, t: (0, 0))
    return pl.pallas_call(
        functools.partial(_lru_kernel, tb=tb),
        grid=(batch, nt),
        in_specs=[
            pl.BlockSpec((tb, LRU_W), lambda b, t: (b * nt + t, C_LX // LRU_W)),
            pl.BlockSpec((tb, LRU_W), lambda b, t: (b * nt + t, C_LY // LRU_W)),
            pl.BlockSpec((CONV_W, LRU_W), lambda b, t: (0, 0)),
            vec, mat, vec, mat, vec, vec,
        ],
        out_specs=pl.BlockSpec((tb, LRU_W), lambda b, t: (b * nt + t, 0)),
        out_shape=jax.ShapeDtypeStruct((n, LRU_W), ACT),
        scratch_shapes=[
            pltpu.VMEM((tb + HALO, LRU_W), F32),
            pltpu.VMEM((tb, LRU_W), F32),
            pltpu.VMEM((tb, LRU_W), F32),
            pltpu.VMEM((8, LRU_W), F32),
        ],
        compiler_params=_cparams(("parallel", "arbitrary")),
        name="rglru",
    )(p, p, cw, cb, wa, ba, wx, bx, lam)


def _top2_sum(a, b, c, d):
    m1, n1 = jnp.maximum(a, b), jnp.minimum(a, b)
    m2, n2 = jnp.maximum(c, d), jnp.minimum(c, d)
    return jnp.maximum(m1, m2) + jnp.maximum(jnp.minimum(m1, m2), jnp.maximum(n1, n2))


def _merge_kernel(x_ref, g0_ref, g1_ref, g2_ref, ya_ref, yb_ref, yc_ref, wa_ref, wb_ref, wc_ref, wo_ref,
                  bm_ref, lng_ref, lnb_ref, wrt_ref, br_ref, x1_ref, gate_ref, gatet_ref, gt_ref):
    merged = (_sigmoid(g0_ref[...].astype(F32) + bm_ref[0:1, :]) * _dot(ya_ref[...].astype(BF16), wa_ref[...])
              + _sigmoid(g1_ref[...].astype(F32) + bm_ref[1:2, :]) * _dot(yb_ref[...].astype(BF16), wb_ref[...])
              + _sigmoid(g2_ref[...].astype(F32) + bm_ref[2:3, :]) * _dot(yc_ref[...].astype(BF16), wc_ref[...]))
    mix = _dot(merged.astype(BF16), wo_ref[...])
    x1 = _layer_norm(ALPHA * x_ref[...] + mix, lng_ref[...], lnb_ref[...])
    x1_ref[...] = x1

    lt = _dot_nt(wrt_ref[...], x1.astype(BF16))
    ex = jnp.exp(lt - jnp.max(lt, axis=0, keepdims=True))
    score = ex / jnp.sum(ex, axis=0, keepdims=True)
    sel = score + br_ref[:, 0:1]
    sc = [score[e:e + 1, :] for e in range(N_EXPERTS)]
    sl = [sel[e:e + 1, :] for e in range(N_EXPERTS)]

    best = _top2_sum(*sl[0:EPG])
    gidx = jnp.zeros_like(best, dtype=jnp.int32)
    for g in range(1, N_GROUPS):
        gs = _top2_sum(*sl[g * EPG:(g + 1) * EPG])
        better = gs > best
        gidx = jnp.where(better, g, gidx)
        best = jnp.where(better, gs, best)

    def pick(vals, j):
        out = vals[j]
        for g in range(1, N_GROUPS):
            out = jnp.where(gidx == g, vals[g * EPG + j], out)
        return out

    vs = [pick(sl, j) for j in range(EPG)]
    ws = [pick(sc, j) for j in range(EPG)]
    b1, i1, w1 = vs[0], jnp.zeros_like(gidx), ws[0]
    for j in range(1, EPG):
        gt = vs[j] > b1
        i1 = jnp.where(gt, j, i1)
        w1 = jnp.where(gt, ws[j], w1)
        b1 = jnp.where(gt, vs[j], b1)
    b2 = jnp.full_like(b1, -jnp.inf)
    i2 = jnp.zeros_like(gidx)
    w2 = jnp.zeros_like(w1)
    for j in range(EPG):
        gt = (i1 != j) & (vs[j] > b2)
        i2 = jnp.where(gt, j, i2)
        w2 = jnp.where(gt, ws[j], w2)
        b2 = jnp.where(gt, vs[j], b2)
    den = w1 + w2
    w1n = w1 / den
    w2n = w2 / den

    gt_ref[...] = jnp.zeros_like(gt_ref)
    for g in range(N_GROUPS):
        ing = gidx == g
        gt_ref[N_EXPERTS + g:N_EXPERTS + g + 1, :] = jnp.where(ing, 1.0, 0.0)
        for j in range(EPG):
            e = g * EPG + j
            gt_ref[e:e + 1, :] = (jnp.where(ing & (i1 == j), w1n, 0.0) + jnp.where(ing & (i2 == j), w2n, 0.0))
    gate_ref[...] = gt_ref[...].T
    gatet_ref[...] = gt_ref[0:GATE_ROWS, :]


def _merge(x, p, ya, yb, yc, wa, wb, wc, wo, bm, lng, lnb, wrt, br, tm):
    n = x.shape[0]
    row = lambda w: pl.BlockSpec((tm, w), lambda i: (i, 0))
    full = lambda a: pl.BlockSpec(a.shape, lambda i: (0, 0))
    mg = C_MG // D_MODEL
    return pl.pallas_call(
        _merge_kernel,
        grid=(n // tm,),
        in_specs=[
            row(D_MODEL),
            pl.BlockSpec((tm, D_MODEL), lambda i: (i, mg)),
            pl.BlockSpec((tm, D_MODEL), lambda i: (i, mg + 1)),
            pl.BlockSpec((tm, D_MODEL), lambda i: (i, mg + 2)),
            row(DN_W), row(SB_W), row(LRU_W),
            full(wa), full(wb), full(wc), full(wo), full(bm), full(lng), full(lnb), full(wrt), full(br),
        ],
        out_specs=[row(D_MODEL), row(LANES), pl.BlockSpec((GATE_ROWS, tm), lambda i: (0, i))],
        out_shape=[jax.ShapeDtypeStruct((n, D_MODEL), F32), jax.ShapeDtypeStruct((n, LANES), F32),
                   jax.ShapeDtypeStruct((GATE_ROWS, n), F32)],
        scratch_shapes=[pltpu.VMEM((LANES, tm), F32)],
        compiler_params=_cparams(("parallel",)),
        name="merge",
    )(x, p, p, p, ya, yb, yc, wa, wb, wc, wo, bm, lng, lnb, wrt, br)


def _expert_ffn(xb, ge, w1_ref, w3_ref, w2_ref):
    h1 = _dot(xb, w1_ref[0])
    h3 = _dot(xb, w3_ref[0])
    return _dot(((h1 * _sigmoid(h1)) * h3 * ge).astype(BF16), w2_ref[0])


def _moe_kernel(x_ref, gate_ref, gatet_ref, w1_ref, w3_ref, w2_ref, lng_ref, lnb_ref, o_ref,
                xb_ref, acc_ref, rkc_ref, rkr_ref, xs_ref, gs_ref, ys_ref, flag_ref, *, tm, ts, cap):
    e = pl.program_id(1)
    g = e // EPG
    j = e % EPG
    nsub = tm // ts

    @pl.when(e == 0)
    def _():
        xb_ref[...] = x_ref[...].astype(BF16)
        acc_ref[...] = jnp.zeros_like(acc_ref)
        ri = lax.broadcasted_iota(jnp.int32, (tm, tm), 0)
        ci = lax.broadcasted_iota(jnp.int32, (tm, tm), 1)
        low = jnp.where((ri >= ci) & ((ri // ts) == (ci // ts)), 1.0, 0.0).astype(BF16)
        rkc_ref[...] = _dot(low, gate_ref[...].astype(BF16))
        rkr = _dot_nt(gatet_ref[...].astype(BF16), low)
        rkr_ref[...] = rkr
        for gg in range(N_GROUPS):
            count = jnp.max(rkr[N_EXPERTS + gg:N_EXPERTS + gg + 1, :])
            flag_ref[gg] = (count > cap).astype(jnp.int32)

    compact = flag_ref[g] == 0

    @pl.when(compact & (j == 0))
    def _():
        pick = lax.broadcasted_iota(jnp.int32, (GATE_ROWS, tm), 0) == N_EXPERTS + g
        member = jnp.sum(jnp.where(pick, gatet_ref[...], 0.0), axis=0, keepdims=True)
        rank = jnp.sum(jnp.where(pick, rkr_ref[...], 0.0), axis=0, keepdims=True) * member
        slot = (lax.broadcasted_iota(jnp.int32, (cap, ts), 0) + 1).astype(F32)
        for s in range(nsub):
            pm = jnp.where(rank[:, s * ts:(s + 1) * ts] == slot, 1.0, 0.0).astype(BF16)
            xs_ref[s * cap:(s + 1) * cap, :] = _dot(pm, xb_ref[s * ts:(s + 1) * ts, :]).astype(BF16)
            g0, g1, g2 = _split3(gate_ref[s * ts:(s + 1) * ts, :])
            gs_ref[s * cap:(s + 1) * cap, :] = _dot(pm, g0) + _dot(pm, g1) + _dot(pm, g2)
        ys_ref[...] = jnp.zeros_like(ys_ref)

    @pl.when(compact)
    def _():
        lane = lax.broadcasted_iota(jnp.int32, (nsub * cap, LANES), 1)
        ge = jnp.sum(jnp.where(lane == e, gs_ref[...], 0.0), axis=1, keepdims=True)
        ys_ref[...] += _expert_ffn(xs_ref[...], ge, w1_ref, w3_ref, w2_ref)

    @pl.when(compact & (j == EPG - 1))
    def _():
        pick = lax.broadcasted_iota(jnp.int32, (tm, LANES), 1) == N_EXPERTS + g
        member = jnp.sum(jnp.where(pick, gate_ref[...], 0.0), axis=1, keepdims=True)
        rank = jnp.sum(jnp.where(pick, rkc_ref[...], 0.0), axis=1, keepdims=True) * member
        slot = (lax.broadcasted_iota(jnp.int32, (ts, cap), 1) + 1).astype(F32)
        for s in range(nsub):
            pt = jnp.where(rank[s * ts:(s + 1) * ts, :] == slot, 1.0, 0.0).astype(BF16)
            acc_ref[s * ts:(s + 1) * ts, :] += _dot(pt, ys_ref[s * cap:(s + 1) * cap, :].astype(BF16))

    @pl.when(jnp.logical_not(compact))
    def _():
        lane = lax.broadcasted_iota(jnp.int32, (tm, LANES), 1)
        ge = jnp.sum(jnp.where(lane == e, gate_ref[...], 0.0), axis=1, keepdims=True)
        acc_ref[...] += _expert_ffn(xb_ref[...], ge, w1_ref, w3_ref, w2_ref)

    @pl.when(e == N_EXPERTS - 1)
    def _():
        o_ref[...] = _layer_norm(ALPHA * x_ref[...] + acc_ref[...], lng_ref[...], lnb_ref[...])


def _moe(x1, gate, gatet, w1, w3, w2, lng, lnb, tm):
    n = x1.shape[0]
    ts = min(MOE_SUBTILE, tm)
    cap = ts * MOE_CAP_NUM // MOE_CAP_DEN
    rows = (tm // ts) * cap
    vec = pl.BlockSpec((1, D_MODEL), lambda i, e: (0, 0))
    return pl.pallas_call(
        functools.partial(_moe_kernel, tm=tm, ts=ts, cap=cap),
        grid=(n // tm, N_EXPERTS),
        in_specs=[
            pl.BlockSpec((tm, D_MODEL), lambda i, e: (i, 0)),
            pl.BlockSpec((tm, LANES), lambda i, e: (i, 0)),
            pl.BlockSpec((GATE_ROWS, tm), lambda i, e: (0, i)),
            pl.BlockSpec((1, D_MODEL, D_FF), lambda i, e: (e, 0, 0)),
            pl.BlockSpec((1, D_MODEL, D_FF), lambda i, e: (e, 0, 0)),
            pl.BlockSpec((1, D_FF, D_MODEL), lambda i, e: (e, 0, 0)),
            vec, vec,
        ],
        out_specs=pl.BlockSpec((tm, D_MODEL), lambda i, e: (i, 0)),
        out_shape=jax.ShapeDtypeStruct((n, D_MODEL), F32),
        scratch_shapes=[
            pltpu.VMEM((tm, D_MODEL), BF16),
            pltpu.VMEM((tm, D_MODEL), F32),
            pltpu.VMEM((tm, LANES), F32),
            pltpu.VMEM((GATE_ROWS, tm), F32),
            pltpu.VMEM((rows, D_MODEL), BF16),
            pltpu.VMEM((rows, LANES), F32),
            pltpu.VMEM((rows, D_MODEL), F32),
            pltpu.SMEM((N_GROUPS,), jnp.int32),
        ],
        compiler_params=_cparams(("parallel", "arbitrary")),
        name="moe",
    )(x1, gate, gatet, w1, w3, w2, lng, lnb)


def _pad_rows(a, rows):
    return jnp.pad(a, ((0, rows - a.shape[0]), (0, 0)))


def _block_diag(w):
    nb, c, _ = w.shape
    eye = jnp.eye(nb, dtype=w.dtype)
    return jnp.einsum("ncd,nm->ncmd", w, eye).reshape(nb * c, nb * c)


def kernel(x, w_in, dn_conv_w, dn_a_log, dn_dt_bias, dn_norm_w, lru_conv_w, lru_conv_b, lru_w_a, lru_b_a,
           lru_w_x, lru_b_x, lru_lambda, w_up_dn, w_up_sb, w_up_lru, w_merge, b_merge, w_out, ln1_g, ln1_b,
           w_router, b_router, moe_w1, moe_w3, moe_w2, ln2_g, ln2_b):
    batch, seq, _ = x.shape
    n = batch * seq
    tb = min(512, seq)
    tq = min(256, seq)
    tm_proj = min(1024, n)
    tm_merge = min(512, n)
    tm_moe = min(1024, n)
    xf = x.reshape(n, D_MODEL)

    wrt = _pad_rows(w_router.T, 16).astype(BF16)
    br = jnp.pad(b_router.reshape(N_EXPERTS, 1), ((0, 0), (0, LANES - 1)))

    for l in range(DEPTH):
        wi = w_in[l]
        w_all = jnp.concatenate(
            [wi[:, IN_OFF["dn"]:IN_OFF["ba"]], w_merge[l], wi[:, IN_OFF["sb"]:IN_OFF["lru"]], wi[:, IN_OFF["lru"]:]],
            axis=1).astype(BF16)
        w_ba8 = wi[:, IN_OFF["ba"]:IN_OFF["sb"]]
        w_ba = jnp.pad(w_ba8, ((0, 0), (0, LANES - 8))).astype(BF16)
        w_bat = _pad_rows(w_ba8.T, 16).astype(BF16)
        p, ba, bat = _proj(xf, w_all, w_ba, w_bat, tm_proj, PROJ_TN)

        pvec = jnp.stack([dn_a_log[l], dn_dt_bias[l]])
        prow = jnp.pad(pvec, ((0, 6), (4, LANES - 8)))
        pcol = jnp.pad(pvec.T, ((4, 8), (0, LANES - 2)))
        ya = _deltanet(p, ba, bat, dn_conv_w[l], prow, pcol, dn_norm_w[l].reshape(1, DN_DK), batch, seq, tb)

        yb = _stickbreak(p, batch, seq, tq)

        yc = _rglru(p, lru_conv_w[l], lru_conv_b[l].reshape(1, LRU_W),
                    _block_diag(lru_w_a[l]).astype(BF16), lru_b_a[l].reshape(1, LRU_W),
                    _block_diag(lru_w_x[l]).astype(BF16), lru_b_x[l].reshape(1, LRU_W),
                    lru_lambda[l].reshape(1, LRU_W), batch, seq, tb)

        x1, gate, gatet = _merge(xf, p, ya, yb, yc,
                          w_up_dn[l].astype(BF16), w_up_sb[l].astype(BF16), w_up_lru[l].astype(BF16),
                          w_out[l].astype(BF16), _pad_rows(b_merge[l].reshape(3, D_MODEL), 8),
                          ln1_g[l].reshape(1, D_MODEL), ln1_b[l].reshape(1, D_MODEL), wrt, br, tm_merge)

        xf = _moe(x1, gate, gatet, moe_w1[l].astype(BF16), moe_w3[l].astype(BF16), moe_w2[l].astype(BF16),
                  ln2_g[l].reshape(1, D_MODEL), ln2_b[l].reshape(1, D_MODEL), tm_moe)

    return xf.reshape(batch, seq, D_MODEL)
```

```python
import functools

import jax
import jax.numpy as jnp
from jax import lax
from jax.experimental import pallas as pl
from jax.experimental.pallas import tpu as pltpu

F32 = jnp.float32
BF16 = jnp.bfloat16

D_MODEL = 1024
DEPTH = 4
DN_HEADS = 4
DN_DK = 128
DN_CHUNK = 64
DN_CHUNKS_PER_STEP = 4
DN_W = DN_HEADS * DN_DK
SB_HEADS = 8
SB_DH = 64
SB_W = SB_HEADS * SB_DH
SB_BLOCK_W = 256
LRU_W = 512
LRU_BLOCKS = 8
LRU_C = 8.0
CONV_W = 4
N_EXPERTS = 16
N_GROUPS = 4
EPG = N_EXPERTS // N_GROUPS
D_FF = 512
ALPHA = (2.0 * DEPTH) ** 0.25
LN_EPS = 1e-5
RMS_EPS = 1e-6

C_DN = 0
C_MG = 2048
C_SBQ = 5120
C_SBK = 5632
C_SBV = 6144
C_LX = 6656
C_LY = 7168
P_COLS = 7680
PROJ_TN = 2560
IN_OFF = {"dn": 0, "ba": 2048, "sb": 2056, "lru": 3592}

GATE_ROWS = 24
MOE_SUBTILE = 512
MOE_CAP_NUM, MOE_CAP_DEN = 3, 8
ACT = BF16
SB_UNDERFLOW = -120.0
LANES = 128
HALO = 8
VMEM_LIMIT = 48 * 1024 * 1024


def _cparams(sem):
    return pltpu.CompilerParams(dimension_semantics=sem, vmem_limit_bytes=VMEM_LIMIT)


def _dot(a, b):
    return jnp.dot(a, b, preferred_element_type=F32)


def _dot_nt(a, b):
    return lax.dot_general(a, b, (((1,), (1,)), ((), ())), preferred_element_type=F32)


def _dot_tn(a, b):
    return lax.dot_general(a, b, (((0,), (0,)), ((), ())), preferred_element_type=F32)


def _split3(a):
    hi = a.astype(BF16)
    r1 = a - hi.astype(F32)
    mid = r1.astype(BF16)
    lo = (r1 - mid.astype(F32)).astype(BF16)
    return hi, mid, lo


def _sigmoid(x):
    return 1.0 / (1.0 + jnp.exp(-x))


def _softplus(x):
    return jnp.maximum(x, 0.0) + jnp.log(1.0 + jnp.exp(-jnp.abs(x)))


def _layer_norm(h, g, b):
    mu = jnp.mean(h, axis=-1, keepdims=True)
    hc = h - mu
    var = jnp.mean(hc * hc, axis=-1, keepdims=True)
    return hc * lax.rsqrt(var + LN_EPS) * g + b


def _pack_kernel(wi_ref, wm_ref, wall_ref, wba_ref, wbat_ref):
    w = wi_ref[0]
    wall_ref[0, :, C_DN:C_MG] = w[:, IN_OFF["dn"]:IN_OFF["ba"]].astype(BF16)
    wall_ref[0, :, C_MG:C_SBQ] = wm_ref[0].astype(BF16)
    wall_ref[0, :, C_SBQ:C_LX] = w[:, IN_OFF["sb"]:IN_OFF["lru"]].astype(BF16)
    wall_ref[0, :, C_LX:P_COLS] = w[:, IN_OFF["lru"]:IN_OFF["lru"] + 2 * LRU_W].astype(BF16)
    lane = lax.broadcasted_iota(jnp.int32, (w.shape[0], LANES), 1)
    ba = jnp.where(lane < 8, w[:, IN_OFF["ba"]:IN_OFF["ba"] + LANES], 0.0)
    wba_ref[0] = ba.astype(BF16)
    wbat_ref[0] = ba.T[0:16, :].astype(BF16)


def _pack_weights(w_in, w_merge):
    depth, d, d_in = w_in.shape
    tr = 256
    return pl.pallas_call(
        _pack_kernel,
        grid=(depth, d // tr),
        in_specs=[
            pl.BlockSpec((1, tr, d_in), lambda l, r: (l, r, 0)),
            pl.BlockSpec((1, tr, w_merge.shape[2]), lambda l, r: (l, r, 0)),
        ],
        out_specs=[
            pl.BlockSpec((1, tr, P_COLS), lambda l, r: (l, r, 0)),
            pl.BlockSpec((1, tr, LANES), lambda l, r: (l, r, 0)),
            pl.BlockSpec((1, 16, tr), lambda l, r: (l, 0, r)),
        ],
        out_shape=[
            jax.ShapeDtypeStruct((depth, d, P_COLS), BF16),
            jax.ShapeDtypeStruct((depth, d, LANES), BF16),
            jax.ShapeDtypeStruct((depth, 16, d), BF16),
        ],
        compiler_params=_cparams(("parallel", "parallel")),
        name="pack_weights",
    )(w_in, w_merge)


def _proj_kernel(x_ref, w_ref, wba_ref, wbat_ref, p_ref, ba_ref, bat_ref, xb_ref):
    @pl.when(pl.program_id(1) == 0)
    def _():
        xb = x_ref[...].astype(BF16)
        xb_ref[...] = xb
        ba_ref[...] = _dot(xb, wba_ref[...])
        bat_ref[...] = _dot_nt(wbat_ref[...], xb)

    p_ref[...] = _dot(xb_ref[...], w_ref[...]).astype(p_ref.dtype)


def _proj(x, w_all, w_ba, w_bat, tm, tn):
    n = x.shape[0]
    return pl.pallas_call(
        _proj_kernel,
        grid=(n // tm, P_COLS // tn),
        in_specs=[
            pl.BlockSpec((tm, D_MODEL), lambda i, j: (i, 0)),
            pl.BlockSpec((D_MODEL, tn), lambda i, j: (0, j)),
            pl.BlockSpec((D_MODEL, LANES), lambda i, j: (0, 0)),
            pl.BlockSpec((16, D_MODEL), lambda i, j: (0, 0)),
        ],
        out_specs=[
            pl.BlockSpec((tm, tn), lambda i, j: (i, j)),
            pl.BlockSpec((tm, LANES), lambda i, j: (i, 0)),
            pl.BlockSpec((16, tm), lambda i, j: (0, i)),
        ],
        out_shape=[
            jax.ShapeDtypeStruct((n, P_COLS), ACT),
            jax.ShapeDtypeStruct((n, LANES), F32),
            jax.ShapeDtypeStruct((16, n), F32),
        ],
        scratch_shapes=[pltpu.VMEM((tm, D_MODEL), BF16)],
        compiler_params=_cparams(("parallel", "arbitrary")),
        name="proj",
    )(x, w_all, w_ba, w_bat)


def _conv_from_pad(xpad_ref, cw, r, rows):
    acc = cw[CONV_W - 1:CONV_W, :] * xpad_ref[r + HALO:r + HALO + rows, :]
    for j in range(CONV_W - 1):
        off = r + HALO - (CONV_W - 1) + j
        acc = acc + cw[j:j + 1, :] * xpad_ref[off:off + rows, :]
    return acc


def _load_halo_block(xpad_ref, blk, tb, first):
    @pl.when(first)
    def _():
        xpad_ref[0:HALO, :] = jnp.zeros((HALO, xpad_ref.shape[1]), F32)

    @pl.when(jnp.logical_not(first))
    def _():
        xpad_ref[0:HALO, :] = xpad_ref[tb:tb + HALO, :]

    xpad_ref[HALO:tb + HALO, :] = blk


def _dn_kernel(pd_ref, ba_ref, bat_ref, cw_ref, prow_ref, pcol_ref, nw_ref, ya_ref,
               xpad_ref, qkv_ref, s_ref, gcc_ref, gct_ref, u_ref, w_ref, qe_ref, kdt_ref, qk_ref, *, tb):
    C = DN_CHUNK
    nc = tb // C
    first = pl.program_id(1) == 0

    @pl.when(first)
    def _():
        s_ref[...] = jnp.zeros_like(s_ref)

    _load_halo_block(xpad_ref, pd_ref[:, 0:3 * DN_W].astype(F32), tb, first)

    cw = cw_ref[...]
    for r in range(0, tb, C):
        y = _conv_from_pad(xpad_ref, cw, r, C)
        y = y * _sigmoid(y)
        for h in range(2 * DN_HEADS):
            seg = y[:, h * DN_DK:(h + 1) * DN_DK]
            nrm = lax.rsqrt(jnp.sum(seg * seg, axis=-1, keepdims=True) + RMS_EPS)
            if h < DN_HEADS:
                nrm = nrm * (DN_DK ** -0.5)
            qkv_ref[r:r + C, h * DN_DK:(h + 1) * DN_DK] = seg * nrm
        qkv_ref[r:r + C, 2 * DN_W:3 * DN_W] = y[:, 2 * DN_W:3 * DN_W]

    g_col = -jnp.exp(prow_ref[0:1, :]) * _softplus(ba_ref[...] + prow_ref[1:2, :])
    g_row = -jnp.exp(pcol_ref[:, 0:1]) * _softplus(bat_ref[...] + pcol_ref[:, 1:2])
    ri = lax.broadcasted_iota(jnp.int32, (tb, tb), 0)
    ci = lax.broadcasted_iota(jnp.int32, (tb, tb), 1)
    same = (ri // C) == (ci // C)
    low = jnp.where(same & (ri >= ci), 1.0, 0.0).astype(BF16)
    upp = jnp.where(same & (ri <= ci), 1.0, 0.0).astype(BF16)
    c0, c1, c2 = _split3(g_col)
    gcc_ref[...] = _dot(low, c0) + _dot(low, c1) + _dot(low, c2)
    r0_, r1_, r2_ = _split3(g_row)
    gct = _dot(r0_, upp) + _dot(r1_, upp) + _dot(r2_, upp)
    for c in range(nc):
        gct_ref[c] = gct[:, c * C:(c + 1) * C]

    rr = lax.broadcasted_iota(jnp.int32, (C, C), 0)
    cc = lax.broadcasted_iota(jnp.int32, (C, C), 1)
    incl = rr >= cc
    strict = rr > cc
    eye = jnp.where(rr == cc, 1.0, 0.0).astype(F32)
    nw = nw_ref[...]

    H = range(DN_HEADS)

    def cols(h, base=0):
        return slice(base + h * DN_DK, base + (h + 1) * DN_DK)

    def phase_a(cg, carry):
        cs = [cg * DN_CHUNKS_PER_STEP + j for j in range(DN_CHUNKS_PER_STEP)]
        rws = [pl.ds(pl.multiple_of(c * C, C), C) for c in cs]
        bsig = [_sigmoid(ba_ref[rows, :]) for rows in rws]
        gcc_c = [gcc_ref[rows, :] for rows in rws]
        gct_c = [gct_ref[c] for c in cs]
        W = [(j, h) for j in range(DN_CHUNKS_PER_STEP) for h in H]
        L = range(len(W))
        q = [qkv_ref[rws[j], cols(h)] for j, h in W]
        k = [qkv_ref[rws[j], cols(h, DN_W)] for j, h in W]
        v = [qkv_ref[rws[j], cols(h, 2 * DN_W)] for j, h in W]
        beta = [bsig[j][:, h:h + 1] for j, h in W]
        gc = [gcc_c[j][:, 4 + h:5 + h] for j, h in W]
        gr = [gct_c[j][4 + h:5 + h, :] for j, h in W]
        decay = [jnp.where(incl, jnp.exp(jnp.where(incl, gc[i] - gr[i], 0.0)), 0.0) for i in L]
        kb = [k[i] * beta[i] for i in L]
        kbf = [k[i].astype(BF16) for i in L]
        kk = [_dot_nt(kb[i].astype(BF16), kbf[i]) for i in L]
        pw = [jnp.where(strict, -(kk[i] * decay[i]), 0.0) for i in L]
        t_mat = [eye + pw[i] for i in L]
        pb = [pw[i].astype(BF16) for i in L]
        for _ in range(5):
            pw = [_dot(pb[i], pb[i]) for i in L]
            pb = [pw[i].astype(BF16) for i in L]
            t_mat = [t_mat[i] + _dot(t_mat[i].astype(BF16), pb[i]) for i in L]
        eg = [jnp.exp(gc[i]) for i in L]
        uw = [_dot(t_mat[i].astype(BF16),
                   jnp.concatenate([v[i] * beta[i], kb[i] * eg[i]], axis=1).astype(BF16)) for i in L]
        qk = [_dot_nt(q[i].astype(BF16), kbf[i]) * decay[i] for i in L]
        for i, (j, h) in enumerate(W):
            u_ref[rws[j], cols(h)] = uw[i][:, 0:DN_DK]
            w_ref[rws[j], cols(h)] = uw[i][:, DN_DK:2 * DN_DK].astype(BF16)
            qe_ref[rws[j], cols(h)] = (q[i] * eg[i]).astype(BF16)
            kd = k[i] * jnp.exp(gr[i][:, C - 1:C] - gc[i])
            kdt_ref[cs[j], h] = kd.T.astype(BF16)
            qk_ref[cs[j], h] = qk[i].astype(BF16)
        return carry

    lax.fori_loop(0, nc // DN_CHUNKS_PER_STEP, phase_a, 0)

    def phase_b(c, carry):
        r0 = pl.multiple_of(c * C, C)
        rows = pl.ds(r0, C)
        gct_c = gct_ref[c]
        s_old = [s_ref[h] for h in H]
        sbf = [s_old[h].astype(BF16) for h in H]
        v_new = [u_ref[rows, cols(h)] - _dot(w_ref[rows, cols(h)], sbf[h]) for h in H]
        vnb = [v_new[h].astype(BF16) for h in H]
        o = [_dot(qe_ref[rows, cols(h)], sbf[h]) + _dot(qk_ref[c, h], vnb[h]) for h in H]
        for h in H:
            s_ref[h] = s_old[h] * jnp.exp(gct_c[4 + h:5 + h, C - 1:C]) + _dot(kdt_ref[c, h], vnb[h])
        for h in H:
            z = pd_ref[rows, cols(h, 3 * DN_W)].astype(F32)
            n = o[h] * lax.rsqrt(jnp.mean(o[h] * o[h], axis=-1, keepdims=True) + RMS_EPS) * nw
            ya_ref[rows, cols(h)] = (n * (z * _sigmoid(z))).astype(ya_ref.dtype)
        return carry

    lax.fori_loop(0, nc, phase_b, 0)


def _deltanet(p, ba, bat, cw, prow, pcol, nw, batch, seq, tb):
    n = batch * seq
    nt = seq // tb
    return pl.pallas_call(
        functools.partial(_dn_kernel, tb=tb),
        grid=(batch, nt),
        in_specs=[
            pl.BlockSpec((tb, 4 * DN_W), lambda b, t: (b * nt + t, C_DN // (4 * DN_W))),
            pl.BlockSpec((tb, LANES), lambda b, t: (b * nt + t, 0)),
            pl.BlockSpec((16, tb), lambda b, t: (0, b * nt + t)),
            pl.BlockSpec((CONV_W, 3 * DN_W), lambda b, t: (0, 0)),
            pl.BlockSpec((8, LANES), lambda b, t: (0, 0)),
            pl.BlockSpec((16, LANES), lambda b, t: (0, 0)),
            pl.BlockSpec((1, DN_DK), lambda b, t: (0, 0)),
        ],
        out_specs=pl.BlockSpec((tb, DN_W), lambda b, t: (b * nt + t, 0)),
        out_shape=jax.ShapeDtypeStruct((n, DN_W), ACT),
        scratch_shapes=[
            pltpu.VMEM((tb + HALO, 3 * DN_W), F32),
            pltpu.VMEM((tb, 3 * DN_W), F32),
            pltpu.VMEM((DN_HEADS, DN_DK, DN_DK), F32),
            pltpu.VMEM((tb, LANES), F32),
            pltpu.VMEM((tb // DN_CHUNK, 16, DN_CHUNK), F32),
            pltpu.VMEM((tb, DN_W), F32),
            pltpu.VMEM((tb, DN_W), BF16),
            pltpu.VMEM((tb, DN_W), BF16),
            pltpu.VMEM((tb // DN_CHUNK, DN_HEADS, DN_DK, DN_CHUNK), BF16),
            pltpu.VMEM((tb // DN_CHUNK, DN_HEADS, DN_CHUNK, DN_CHUNK), BF16),
        ],
        compiler_params=_cparams(("parallel", "arbitrary")),
        name="deltanet",
    )(p, ba, bat, cw, prow, pcol, nw)


def _sb_kernel(q_ref, k_ref, v_ref, o_ref, qh_ref, car_ref, acc_ref, *, tq):
    qi = pl.program_id(2)
    lane = lax.broadcasted_iota(jnp.int32, (tq, SB_BLOCK_W), 1)
    q = q_ref[...].astype(F32) * (SB_DH ** -0.5)
    ri = lax.broadcasted_iota(jnp.int32, (tq, tq), 0)
    ci = lax.broadcasted_iota(jnp.int32, (tq, tq), 1)
    past = ci < ri
    suf1 = jnp.where(ri > ci, 1.0, 0.0).astype(BF16)
    suf = jnp.concatenate([suf1, suf1], axis=0)
    HH = range(SB_BLOCK_W // SB_DH)
    hms = [(lane >= h * SB_DH) & (lane < (h + 1) * SB_DH) for h in HH]
    for h in HH:
        qh_ref[h] = jnp.where(hms[h], q, 0.0).astype(BF16)
    car_ref[...] = jnp.zeros_like(car_ref)
    acc_ref[...] = jnp.zeros_like(acc_ref)

    def blk(kb, masked):
        k0 = pl.multiple_of(kb * tq, tq)
        kk = k_ref[pl.ds(k0, tq), :].astype(BF16)
        vv = v_ref[pl.ds(k0, tq), :].astype(BF16)
        z = [_dot_nt(qh_ref[h], kk) for h in HH]
        sp = [_softplus(z[h]) for h in HH]
        ls = [-sp[h] for h in HH]
        if masked:
            ls = [jnp.where(past, ls[h], 0.0) for h in HH]
        hi = [ls[h].astype(BF16) for h in HH]
        lo = [(ls[h] - hi[h].astype(F32)).astype(BF16) for h in HH]
        after = [_dot(jnp.concatenate([hi[h], lo[h]], axis=1), suf) for h in HH]
        att = [jnp.exp((z[h] - sp[h]) + after[h] + car_ref[h]) for h in HH]
        if masked:
            att = [jnp.where(past, att[h], 0.0) for h in HH]
        for h in HH:
            acc_ref[h] += _dot(att[h].astype(BF16), vv)
            car_ref[h] += jnp.sum(ls[h], axis=1, keepdims=True)
        top = car_ref[0]
        for h in HH[1:]:
            top = jnp.maximum(top, car_ref[h])
        return jnp.max(top)

    def cond(st):
        kb, top = st
        return (kb >= 0) & (top > SB_UNDERFLOW)

    def body(st):
        kb, _ = st
        return kb - 1, blk(kb, False)

    lax.while_loop(cond, body, (qi - 1, blk(qi, True)))
    out = acc_ref[0]
    for h in HH[1:]:
        out = jnp.where(hms[h], acc_ref[h], out)
    o_ref[...] = out.astype(o_ref.dtype)


def _stickbreak(p, batch, seq, tq):
    n = batch * seq
    nq = seq // tq
    bw = SB_BLOCK_W
    nh = bw // SB_DH
    return pl.pallas_call(
        functools.partial(_sb_kernel, tq=tq),
        grid=(batch, SB_W // bw, nq),
        in_specs=[
            pl.BlockSpec((tq, bw), lambda b, h, i: (b * nq + i, C_SBQ // bw + h)),
            pl.BlockSpec((seq, bw), lambda b, h, i: (b, C_SBK // bw + h)),
            pl.BlockSpec((seq, bw), lambda b, h, i: (b, C_SBV // bw + h)),
        ],
        out_specs=pl.BlockSpec((tq, bw), lambda b, h, i: (b * nq + i, h)),
        out_shape=jax.ShapeDtypeStruct((n, SB_W), ACT),
        scratch_shapes=[
            pltpu.VMEM((nh, tq, bw), BF16),
            pltpu.VMEM((nh, tq, 1), F32),
            pltpu.VMEM((nh, tq, bw), F32),
        ],
        compiler_params=_cparams(("parallel", "parallel", "arbitrary")),
        name="stickbreak",
    )(p, p, p)


def _lru_kernel(px_ref, py_ref, cw_ref, cb_ref, wa_ref, ba_ref, wx_ref, bx_ref, lam_ref, yc_ref,
                xpad_ref, a_ref, h_ref, car_ref, *, tb):
    first = pl.program_id(1) == 0

    @pl.when(first)
    def _():
        car_ref[...] = jnp.zeros_like(car_ref)

    _load_halo_block(xpad_ref, px_ref[...].astype(F32), tb, first)
    xc = _conv_from_pad(xpad_ref, cw_ref[...], 0, tb) + cb_ref[...]
    xcb = xc.astype(BF16)
    r = _sigmoid(_dot(xcb, wa_ref[...]) + ba_ref[...])
    i = _sigmoid(_dot(xcb, wx_ref[...]) + bx_ref[...])
    log_a = -LRU_C * r * _softplus(-lam_ref[...])
    a_ref[...] = jnp.exp(log_a)
    h_ref[...] = jnp.sqrt(1.0 - jnp.exp(2.0 * log_a)) * (i * xc)

    row = lax.broadcasted_iota(jnp.int32, (8, LRU_W), 0)

    def tile(j, h):
        r0 = pl.multiple_of(j * 8, 8)
        a = a_ref[pl.ds(r0, 8), :]
        b = h_ref[pl.ds(r0, 8), :]
        for s in (1, 2, 4):
            m = row >= s
            b = jnp.where(m, a * pltpu.roll(b, s, axis=0) + b, b)
            a = jnp.where(m, a * pltpu.roll(a, s, axis=0), a)
        hh = a * h + b
        h_ref[pl.ds(r0, 8), :] = hh
        return hh[7:8, :]

    car_ref[0:1, :] = lax.fori_loop(0, tb // 8, tile, car_ref[0:1, :])
    y = py_ref[...].astype(F32)
    gelu = 0.5 * y * (1.0 + jnp.tanh(0.7978845608028654 * (y + 0.044715 * (y * y * y))))
    yc_ref[...] = (gelu * h_ref[...]).astype(yc_ref.dtype)


def _rglru(p, cw, cb, wa, ba, wx, bx, lam, batch, seq, tb):
    n = batch * seq
    nt = seq // tb
    vec = pl.BlockSpec((1, LRU_W), lambda b, t: (0, 0))
    mat = pl.BlockSpec((LRU_W, LRU_W), lambda b, t: (0, 0))
    return pl.pallas_call(
        functools.partial(_lru_kernel, tb=tb),
        grid=(batch, nt),
        in_specs=[
            pl.BlockSpec((tb, LRU_W), lambda b, t: (b * nt + t, C_LX // LRU_W)),
            pl.BlockSpec((tb, LRU_W), lambda b, t: (b * nt + t, C_LY // LRU_W)),
            pl.BlockSpec((CONV_W, LRU_W), lambda b, t: (0, 0)),
            vec, mat, vec, mat, vec, vec,
        ],
        out_specs=pl.BlockSpec((tb, LRU_W), lambda b, t: (b * nt + t, 0)),
        out_shape=jax.ShapeDtypeStruct((n, LRU_W), ACT),
        scratch_shapes=[
            pltpu.VMEM((tb + HALO, LRU_W), F32),
            pltpu.VMEM((tb, LRU_W), F32),
            pltpu.VMEM((tb, LRU_W), F32),
            pltpu.VMEM((8, LRU_W), F32),
        ],
        compiler_params=_cparams(("parallel", "arbitrary")),
        name="rglru",
    )(p, p, cw, cb, wa, ba, wx, bx, lam)


def _top2_sum(a, b, c, d):
    m1, n1 = jnp.maximum(a, b), jnp.minimum(a, b)
    m2, n2 = jnp.maximum(c, d), jnp.minimum(c, d)
    return jnp.maximum(m1, m2) + jnp.maximum(jnp.minimum(m1, m2), jnp.maximum(n1, n2))


def _merge_kernel(x_ref, g0_ref, g1_ref, g2_ref, ya_ref, yb_ref, yc_ref, wa_ref, wb_ref, wc_ref, wo_ref,
                  bm_ref, lng_ref, lnb_ref, wrt_ref, br_ref, x1_ref, gate_ref, gatet_ref, gt_ref):
    merged = (_sigmoid(g0_ref[...].astype(F32) + bm_ref[0:1, :]) * _dot(ya_ref[...].astype(BF16), wa_ref[...])
              + _sigmoid(g1_ref[...].astype(F32) + bm_ref[1:2, :]) * _dot(yb_ref[...].astype(BF16), wb_ref[...])
              + _sigmoid(g2_ref[...].astype(F32) + bm_ref[2:3, :]) * _dot(yc_ref[...].astype(BF16), wc_ref[...]))
    mix = _dot(merged.astype(BF16), wo_ref[...])
    x1 = _layer_norm(ALPHA * x_ref[...] + mix, lng_ref[...], lnb_ref[...])
    x1_ref[...] = x1

    lt = _dot_nt(wrt_ref[...], x1.astype(BF16))
    ex = jnp.exp(lt - jnp.max(lt, axis=0, keepdims=True))
    score = ex / jnp.sum(ex, axis=0, keepdims=True)
    sel = score + br_ref[:, 0:1]
    sc = [score[e:e + 1, :] for e in range(N_EXPERTS)]
    sl = [sel[e:e + 1, :] for e in range(N_EXPERTS)]

    best = _top2_sum(*sl[0:EPG])
    gidx = jnp.zeros_like(best, dtype=jnp.int32)
    for g in range(1, N_GROUPS):
        gs = _top2_sum(*sl[g * EPG:(g + 1) * EPG])
        better = gs > best
        gidx = jnp.where(better, g, gidx)
        best = jnp.where(better, gs, best)

    def pick(vals, j):
        out = vals[j]
        for g in range(1, N_GROUPS):
            out = jnp.where(gidx == g, vals[g * EPG + j], out)
        return out

    vs = [pick(sl, j) for j in range(EPG)]
    ws = [pick(sc, j) for j in range(EPG)]
    b1, i1, w1 = vs[0], jnp.zeros_like(gidx), ws[0]
    for j in range(1, EPG):
        gt = vs[j] > b1
        i1 = jnp.where(gt, j, i1)
        w1 = jnp.where(gt, ws[j], w1)
        b1 = jnp.where(gt, vs[j], b1)
    b2 = jnp.full_like(b1, -jnp.inf)
    i2 = jnp.zeros_like(gidx)
    w2 = jnp.zeros_like(w1)
    for j in range(EPG):
        gt = (i1 != j) & (vs[j] > b2)
        i2 = jnp.where(gt, j, i2)
        w2 = jnp.where(gt, ws[j], w2)
        b2 = jnp.where(gt, vs[j], b2)
    den = w1 + w2
    w1n = w1 / den
    w2n = w2 / den

    gt_ref[...] = jnp.zeros_like(gt_ref)
    for g in range(N_GROUPS):
        ing = gidx == g
        gt_ref[N_EXPERTS + g:N_EXPERTS + g + 1, :] = jnp.where(ing, 1.0, 0.0)
        for j in range(EPG):
            e = g * EPG + j
            gt_ref[e:e + 1, :] = (jnp.where(ing & (i1 == j), w1n, 0.0) + jnp.where(ing & (i2 == j), w2n, 0.0))
    gate_ref[...] = gt_ref[...].T
    gatet_ref[...] = gt_ref[0:GATE_ROWS, :]


def _merge(x, p, ya, yb, yc, wa, wb, wc, wo, bm, lng, lnb, wrt, br, tm):
    n = x.shape[0]
    row = lambda w: pl.BlockSpec((tm, w), lambda i: (i, 0))
    full = lambda a: pl.BlockSpec(a.shape, lambda i: (0, 0))
    mg = C_MG // D_MODEL
    return pl.pallas_call(
        _merge_kernel,
        grid=(n // tm,),
        in_specs=[
            row(D_MODEL),
            pl.BlockSpec((tm, D_MODEL), lambda i: (i, mg)),
            pl.BlockSpec((tm, D_MODEL), lambda i: (i, mg + 1)),
            pl.BlockSpec((tm, D_MODEL), lambda i: (i, mg + 2)),
            row(DN_W), row(SB_W), row(LRU_W),
            full(wa), full(wb), full(wc), full(wo), full(bm), full(lng), full(lnb), full(wrt), full(br),
        ],
        out_specs=[row(D_MODEL), row(LANES), pl.BlockSpec((GATE_ROWS, tm), lambda i: (0, i))],
        out_shape=[jax.ShapeDtypeStruct((n, D_MODEL), F32), jax.ShapeDtypeStruct((n, LANES), F32),
                   jax.ShapeDtypeStruct((GATE_ROWS, n), F32)],
        scratch_shapes=[pltpu.VMEM((LANES, tm), F32)],
        compiler_params=_cparams(("parallel",)),
        name="merge",
    )(x, p, p, p, ya, yb, yc, wa, wb, wc, wo, bm, lng, lnb, wrt, br)


def _expert_ffn(xb, ge, w1_ref, w3_ref, w2_ref):
    h1 = _dot(xb, w1_ref[0])
    h3 = _dot(xb, w3_ref[0])
    return _dot(((h1 * _sigmoid(h1)) * h3 * ge).astype(BF16), w2_ref[0])


def _moe_kernel(x_ref, gate_ref, gatet_ref, w1_ref, w3_ref, w2_ref, lng_ref, lnb_ref, o_ref,
                xb_ref, acc_ref, rkc_ref, rkr_ref, xs_ref, gs_ref, ys_ref, flag_ref, *, tm, ts, cap):
    e = pl.program_id(1)
    g = e // EPG
    j = e % EPG
    nsub = tm // ts

    @pl.when(e == 0)
    def _():
        xb_ref[...] = x_ref[...].astype(BF16)
        acc_ref[...] = jnp.zeros_like(acc_ref)
        ri = lax.broadcasted_iota(jnp.int32, (tm, tm), 0)
        ci = lax.broadcasted_iota(jnp.int32, (tm, tm), 1)
        low = jnp.where((ri >= ci) & ((ri // ts) == (ci // ts)), 1.0, 0.0).astype(BF16)
        rkc_ref[...] = _dot(low, gate_ref[...].astype(BF16))
        rkr = _dot_nt(gatet_ref[...].astype(BF16), low)
        rkr_ref[...] = rkr
        for gg in range(N_GROUPS):
            count = jnp.max(rkr[N_EXPERTS + gg:N_EXPERTS + gg + 1, :])
            flag_ref[gg] = (count > cap).astype(jnp.int32)

    compact = flag_ref[g] == 0

    @pl.when(compact & (j == 0))
    def _():
        pick = lax.broadcasted_iota(jnp.int32, (GATE_ROWS, tm), 0) == N_EXPERTS + g
        member = jnp.sum(jnp.where(pick, gatet_ref[...], 0.0), axis=0, keepdims=True)
        rank = jnp.sum(jnp.where(pick, rkr_ref[...], 0.0), axis=0, keepdims=True) * member
        slot = (lax.broadcasted_iota(jnp.int32, (cap, ts), 0) + 1).astype(F32)
        for s in range(nsub):
            pm = jnp.where(rank[:, s * ts:(s + 1) * ts] == slot, 1.0, 0.0).astype(BF16)
            xs_ref[s * cap:(s + 1) * cap, :] = _dot(pm, xb_ref[s * ts:(s + 1) * ts, :]).astype(BF16)
            g0, g1, g2 = _split3(gate_ref[s * ts:(s + 1) * ts, :])
            gs_ref[s * cap:(s + 1) * cap, :] = _dot(pm, g0) + _dot(pm, g1) + _dot(pm, g2)
        ys_ref[...] = jnp.zeros_like(ys_ref)

    @pl.when(compact)
    def _():
        lane = lax.broadcasted_iota(jnp.int32, (nsub * cap, LANES), 1)
        ge = jnp.sum(jnp.where(lane == e, gs_ref[...], 0.0), axis=1, keepdims=True)
        ys_ref[...] += _expert_ffn(xs_ref[...], ge, w1_ref, w3_ref, w2_ref)

    @pl.when(compact & (j == EPG - 1))
    def _():
        pick = lax.broadcasted_iota(jnp.int32, (tm, LANES), 1) == N_EXPERTS + g
        member = jnp.sum(jnp.where(pick, gate_ref[...], 0.0), axis=1, keepdims=True)
        rank = jnp.sum(jnp.where(pick, rkc_ref[...], 0.0), axis=1, keepdims=True) * member
        slot = (lax.broadcasted_iota(jnp.int32, (ts, cap), 1) + 1).astype(F32)
        for s in range(nsub):
            pt = jnp.where(rank[s * ts:(s + 1) * ts, :] == slot, 1.0, 0.0).astype(BF16)
            acc_ref[s * ts:(s + 1) * ts, :] += _dot(pt, ys_ref[s * cap:(s + 1) * cap, :].astype(BF16))

    @pl.when(jnp.logical_not(compact))
    def _():
        lane = lax.broadcasted_iota(jnp.int32, (tm, LANES), 1)
        ge = jnp.sum(jnp.where(lane == e, gate_ref[...], 0.0), axis=1, keepdims=True)
        acc_ref[...] += _expert_ffn(xb_ref[...], ge, w1_ref, w3_ref, w2_ref)

    @pl.when(e == N_EXPERTS - 1)
    def _():
        o_ref[...] = _layer_norm(ALPHA * x_ref[...] + acc_ref[...], lng_ref[...], lnb_ref[...])


def _moe(x1, gate, gatet, w1, w3, w2, lng, lnb, tm):
    n = x1.shape[0]
    ts = min(MOE_SUBTILE, tm)
    cap = ts * MOE_CAP_NUM // MOE_CAP_DEN
    rows = (tm // ts) * cap
    vec = pl.BlockSpec((1, D_MODEL), lambda i, e: (0, 0))
    return pl.pallas_call(
        functools.partial(_moe_kernel, tm=tm, ts=ts, cap=cap),
        grid=(n // tm, N_EXPERTS),
        in_specs=[
            pl.BlockSpec((tm, D_MODEL), lambda i, e: (i, 0)),
            pl.BlockSpec((tm, LANES), lambda i, e: (i, 0)),
            pl.BlockSpec((GATE_ROWS, tm), lambda i, e: (0, i)),
            pl.BlockSpec((1, D_MODEL, D_FF), lambda i, e: (e, 0, 0)),
            pl.BlockSpec((1, D_MODEL, D_FF), lambda i, e: (e, 0, 0)),
            pl.BlockSpec((1, D_FF, D_MODEL), lambda i, e: (e, 0, 0)),
            vec, vec,
        ],
        out_specs=pl.BlockSpec((tm, D_MODEL), lambda i, e: (i, 0)),
        out_shape=jax.ShapeDtypeStruct((n, D_MODEL), F32),
        scratch_shapes=[
            pltpu.VMEM((tm, D_MODEL), BF16),
            pltpu.VMEM((tm, D_MODEL), F32),
            pltpu.VMEM((tm, LANES), F32),
            pltpu.VMEM((GATE_ROWS, tm), F32),
            pltpu.VMEM((rows, D_MODEL), BF16),
            pltpu.VMEM((rows, LANES), F32),
            pltpu.VMEM((rows, D_MODEL), F32),
            pltpu.SMEM((N_GROUPS,), jnp.int32),
        ],
        compiler_params=_cparams(("parallel", "arbitrary")),
        name="moe",
    )(x1, gate, gatet, w1, w3, w2, lng, lnb)


def _pad_rows(a, rows):
    return jnp.pad(a, ((0, rows - a.shape[0]), (0, 0)))


def _block_diag(w):
    nb, c, _ = w.shape
    eye = jnp.eye(nb, dtype=w.dtype)
    return jnp.einsum("ncd,nm->ncmd", w, eye).reshape(nb * c, nb * c)


def kernel(x, w_in, dn_conv_w, dn_a_log, dn_dt_bias, dn_norm_w, lru_conv_w, lru_conv_b, lru_w_a, lru_b_a,
           lru_w_x, lru_b_x, lru_lambda, w_up_dn, w_up_sb, w_up_lru, w_merge, b_merge, w_out, ln1_g, ln1_b,
           w_router, b_router, moe_w1, moe_w3, moe_w2, ln2_g, ln2_b):
    batch, seq, _ = x.shape
    n = batch * seq
    tb = min(512, seq)
    tq = min(256, seq)
    tm_proj = min(1024, n)
    tm_merge = min(512, n)
    tm_moe = min(1024, n)
    xf = x.reshape(n, D_MODEL)

    wrt = _pad_rows(w_router.T, 16).astype(BF16)
    br = jnp.pad(b_router.reshape(N_EXPERTS, 1), ((0, 0), (0, LANES - 1)))

    w_all_l, w_ba_l, w_bat_l = _pack_weights(w_in, w_merge)

    for l in range(DEPTH):
        p, ba, bat = _proj(xf, w_all_l[l], w_ba_l[l], w_bat_l[l], tm_proj, PROJ_TN)

        pvec = jnp.stack([dn_a_log[l], dn_dt_bias[l]])
        prow = jnp.pad(pvec, ((0, 6), (4, LANES - 8)))
        pcol = jnp.pad(pvec.T, ((4, 8), (0, LANES - 2)))
        ya = _deltanet(p, ba, bat, dn_conv_w[l], prow, pcol, dn_norm_w[l].reshape(1, DN_DK), batch, seq, tb)

        yb = _stickbreak(p, batch, seq, tq)

        yc = _rglru(p, lru_conv_w[l], lru_conv_b[l].reshape(1, LRU_W),
                    _block_diag(lru_w_a[l]).astype(BF16), lru_b_a[l].reshape(1, LRU_W),
                    _block_diag(lru_w_x[l]).astype(BF16), lru_b_x[l].reshape(1, LRU_W),
                    lru_lambda[l].reshape(1, LRU_W), batch, seq, tb)

        x1, gate, gatet = _merge(xf, p, ya, yb, yc,
                          w_up_dn[l].astype(BF16), w_up_sb[l].astype(BF16), w_up_lru[l].astype(BF16),
                          w_out[l].astype(BF16), _pad_rows(b_merge[l].reshape(3, D_MODEL), 8),
                          ln1_g[l].reshape(1, D_MODEL), ln1_b[l].reshape(1, D_MODEL), wrt, br, tm_merge)

        xf = _moe(x1, gate, gatet, moe_w1[l].astype(BF16), moe_w3[l].astype(BF16), moe_w2[l].astype(BF16),
                  ln2_g[l].reshape(1, D_MODEL), ln2_b[l].reshape(1, D_MODEL), tm_moe)

    return xf.reshape(batch, seq, D_MODEL)
```
